```python
import math
import jax, jax.numpy as jnp
from jax import lax
import numpy as np

D_MODEL = 1024
BATCH = 2
SEQ = 8192
DEPTH = 2
DEC_BATCH = 16
DEC_SEQ = 64
PAST_LEN = 2048

CHUNK = 64
N_BAND_PAST = 8
BAND_PAST = N_BAND_PAST * CHUNK
N_HEADS_A = 8
HEAD_DIM = 64
D_ATTN = N_HEADS_A * HEAD_DIM
MAX_REL = 256
N_REL = 2 * MAX_REL + 1
D_RNN = 512
N_RNN_BLOCKS = 8
RNN_BLOCK = D_RNN // N_RNN_BLOCKS
CONV_W = 4
RG_C = 8.0
D_FF = 2816
FFN_CONV_W = 3
EPS = 1e-6
NEG_INF = -1e30
D_IN = 3 * D_ATTN + 2 * D_RNN + 2 * D_MODEL
SPLIT_POINTS = (D_ATTN, 2 * D_ATTN, 3 * D_ATTN, 3 * D_ATTN + D_RNN,
                3 * D_ATTN + 2 * D_RNN, 3 * D_ATTN + 2 * D_RNN + D_MODEL)

kernel_name = "hybrid_chunk_band_rglru_convffn_step"


def rmsnorm(x, g):
    xf = x.astype(jnp.float32)
    y = xf * lax.rsqrt(jnp.mean(xf * xf, axis=-1, keepdims=True) + EPS)
    return (y * g.astype(jnp.float32)).astype(x.dtype)


def adaln(c, w, b):
    m = jax.nn.silu(c) @ w + b
    return jnp.split(m, 3, axis=-1)


def modulate(x, g, shift, scale):
    return rmsnorm(x, g) * (1 + scale[:, None, :]) + shift[:, None, :]


def causal_dwconv(x, prev, w, b):
    width = w.shape[0]
    T = x.shape[1]
    xp = jnp.concatenate([prev, x], axis=1)
    y = b + sum(xp[:, j:j + T] * w[j] for j in range(width))
    return y, xp[:, xp.shape[1] - (width - 1):]


def band_bias(table, n_q, n_past, n_k):
    rel = n_past + jnp.arange(n_q)[:, None] - jnp.arange(n_k)[None, :]
    idx = jnp.clip(rel, -MAX_REL, MAX_REL) + MAX_REL
    return jnp.transpose(table[idx], (2, 0, 1))


def band_attend(q, kb, vb, bias, key_valid):
    s = jnp.einsum('bnthd,bnmhd->bnhtm', q, kb).astype(jnp.float32) * (HEAD_DIM ** -0.5)
    s = s + bias.astype(jnp.float32)[None, None]
    s = jnp.where(key_valid[None, :, None, None, :], s, NEG_INF)
    p = jax.nn.softmax(s, axis=-1).astype(vb.dtype)
    return jnp.einsum('bnhtm,bnmhd->bnthd', p, vb)


def mixer_a_prompt(q, k, v, table):
    B, S, H, dh = q.shape
    nc = S // CHUNK
    nb = N_BAND_PAST + 1
    pad = jnp.zeros((B, BAND_PAST, H, dh), k.dtype)
    kc = jnp.concatenate([pad, k], axis=1).reshape(B, nc + N_BAND_PAST, CHUNK, H, dh)
    vc = jnp.concatenate([pad, v], axis=1).reshape(B, nc + N_BAND_PAST, CHUNK, H, dh)
    idx = jnp.arange(nc)[:, None] + jnp.arange(nb)[None, :]
    kb = kc[:, idx].reshape(B, nc, nb * CHUNK, H, dh)
    vb = vc[:, idx].reshape(B, nc, nb * CHUNK, H, dh)
    valid = jnp.repeat(idx >= N_BAND_PAST, CHUNK, axis=1)
    bias = band_bias(table, CHUNK, BAND_PAST, nb * CHUNK)
    o = band_attend(q.reshape(B, nc, CHUNK, H, dh), kb, vb, bias, valid)
    rows = min(BAND_PAST, S)
    return o.reshape(B, S, H * dh), k[:, S - rows:], v[:, S - rows:]


def mixer_a_sample(q, k, v, past_k, past_v, table):
    B, T, H, dh = q.shape
    P = past_k.shape[1]
    kk = jnp.concatenate([past_k, k], axis=1)[:, None]
    vv = jnp.concatenate([past_v, v], axis=1)[:, None]
    valid = jnp.ones((1, P + T), dtype=bool)
    bias = band_bias(table, T, P, P + T)
    o = band_attend(q[:, None], kk, vv, bias, valid)[:, 0]
    return o.reshape(B, T, H * dh), k, v


def _lin_combine(left, right):
    a1, b1 = left
    a2, b2 = right
    return a1 * a2, a2 * b1 + b2


def rg_lru(x, h0, wa, ba, wx, bx, lam):
    B, T, _ = x.shape
    xb = x.reshape(B, T, N_RNN_BLOCKS, RNN_BLOCK)
    r = jax.nn.sigmoid((jnp.einsum('btni,nij->btnj', xb, wa).reshape(B, T, D_RNN) + ba).astype(jnp.float32))
    ig = jax.nn.sigmoid((jnp.einsum('btni,nij->btnj', xb, wx).reshape(B, T, D_RNN) + bx).astype(jnp.float32))
    log_a = -RG_C * r * jax.nn.softplus(-lam.astype(jnp.float32))
    a = jnp.exp(log_a)
    u = jnp.sqrt(-jnp.expm1(2.0 * log_a)) * (ig * x.astype(jnp.float32))
    u = u.at[:, 0].add(a[:, 0] * h0.astype(jnp.float32))
    _, hs = lax.associative_scan(_lin_combine, (a, u), axis=1)
    return hs.astype(x.dtype), hs[:, -1].astype(x.dtype)


def setup_inputs(seed: int = 0) -> dict:
    key = jax.random.key(seed)
    ks = iter(jax.random.split(key, 40))
    f32 = jnp.float32
    nrm = lambda shape, s: s * jax.random.normal(next(ks), shape, f32)
    a_rows = min(BAND_PAST, PAST_LEN)
    u = jax.random.uniform(next(ks), (DEPTH, D_RNN), f32, 0.9, 0.999)
    a0 = u ** (1.0 / RG_C)
    lam = jnp.log(a0) - jnp.log1p(-a0)
    return {
        "x_prompt": nrm((BATCH, SEQ, D_MODEL), 1.0),
        "x_sample": nrm((DEC_BATCH, DEC_SEQ, D_MODEL), 1.0),
        "c_prompt": nrm((BATCH, D_MODEL), 1.0),
        "c_sample": nrm((DEC_BATCH, D_MODEL), 1.0),
        "cache_k": nrm((DEPTH, DEC_BATCH, a_rows, N_HEADS_A, HEAD_DIM), 1.0),
        "cache_v": nrm((DEPTH, DEC_BATCH, a_rows, N_HEADS_A, HEAD_DIM), 1.0),
        "state_rnn_conv": nrm((DEPTH, DEC_BATCH, CONV_W - 1, D_RNN), 1.0),
        "state_rnn_h": nrm((DEPTH, DEC_BATCH, D_RNN), 0.5),
        "state_ffn_conv": nrm((DEPTH, DEC_BATCH, FFN_CONV_W - 1, 2 * D_FF), 1.0),
        "mod_mix_w": nrm((DEPTH, D_MODEL, 3 * D_MODEL), 0.5 * D_MODEL ** -0.5),
        "mod_mix_b": nrm((DEPTH, 3 * D_MODEL), 0.02),
        "norm_mix_g": 1.0 + nrm((DEPTH, D_MODEL), 0.05),
        "w_in": nrm((DEPTH, D_MODEL, D_IN), D_MODEL ** -0.5),
        "rel_bias_table": nrm((DEPTH, N_REL, N_HEADS_A), 0.2),
        "rnn_conv_w": nrm((DEPTH, CONV_W, D_RNN), CONV_W ** -0.5),
        "rnn_conv_b": nrm((DEPTH, D_RNN), 0.02),
        "rnn_gate_a_w": nrm((DEPTH, N_RNN_BLOCKS, RNN_BLOCK, RNN_BLOCK), RNN_BLOCK ** -0.5),
        "rnn_gate_a_b": nrm((DEPTH, D_RNN), 0.02),
        "rnn_gate_x_w": nrm((DEPTH, N_RNN_BLOCKS, RNN_BLOCK, RNN_BLOCK), RNN_BLOCK ** -0.5),
        "rnn_gate_x_b": nrm((DEPTH, D_RNN), 0.02),
        "rnn_lambda": lam,
        "w_branch": nrm((DEPTH, D_ATTN + D_RNN, D_MODEL), D_RNN ** -0.5),
        "w_out": nrm((DEPTH, D_MODEL, D_MODEL), D_MODEL ** -0.5),
        "mod_ffn_w": nrm((DEPTH, D_MODEL, 3 * D_MODEL), 0.5 * D_MODEL ** -0.5),
        "mod_ffn_b": nrm((DEPTH, 3 * D_MODEL), 0.02),
        "norm_ffn_g": 1.0 + nrm((DEPTH, D_MODEL), 0.05),
        "ffn_up_w": nrm((DEPTH, D_MODEL, 2 * D_FF), D_MODEL ** -0.5),
        "ffn_conv_w": nrm((DEPTH, FFN_CONV_W, 2 * D_FF), FFN_CONV_W ** -0.5),
        "ffn_conv_b": nrm((DEPTH, 2 * D_FF), 0.02),
        "ffn_down_w": nrm((DEPTH, D_FF, D_MODEL), D_FF ** -0.5),
        "final_norm_g": 1.0 + nrm((D_MODEL,), 0.05),
    }


def reference(x_prompt, x_sample, c_prompt, c_sample, cache_k, cache_v, state_rnn_conv,
              state_rnn_h, state_ffn_conv, mod_mix_w, mod_mix_b, norm_mix_g, w_in,
              rel_bias_table, rnn_conv_w, rnn_conv_b, rnn_gate_a_w, rnn_gate_a_b,
              rnn_gate_x_w, rnn_gate_x_b, rnn_lambda, w_branch, w_out, mod_ffn_w, mod_ffn_b,
              norm_ffn_g, ffn_up_w, ffn_conv_w, ffn_conv_b, ffn_down_w, final_norm_g):

    def run(x, c, past_k, past_v, conv_r, h_r, conv_f):
        B, T, _ = x.shape
        ks_, vs_, crs, hs_, cfs = [], [], [], [], []
        for l in range(DEPTH):
            shift, scale, gate = adaln(c, mod_mix_w[l], mod_mix_b[l])
            h = modulate(x, norm_mix_g[l], shift, scale)
            q, k, v, rx, rg, ga, gb = jnp.split(h @ w_in[l], SPLIT_POINTS, axis=-1)
            q = q.reshape(B, T, N_HEADS_A, HEAD_DIM)
            k = k.reshape(B, T, N_HEADS_A, HEAD_DIM)
            v = v.reshape(B, T, N_HEADS_A, HEAD_DIM)
            if past_k is None:
                attn, k_new, v_new = mixer_a_prompt(q, k, v, rel_bias_table[l])
                cr_prev = jnp.zeros((B, CONV_W - 1, D_RNN), x.dtype)
                h_prev = jnp.zeros((B, D_RNN), x.dtype)
                cf_prev = jnp.zeros((B, FFN_CONV_W - 1, 2 * D_FF), x.dtype)
            else:
                attn, k_new, v_new = mixer_a_sample(q, k, v, past_k[l], past_v[l], rel_bias_table[l])
                cr_prev, h_prev, cf_prev = conv_r[l], h_r[l], conv_f[l]
            xc, cr_new = causal_dwconv(rx, cr_prev, rnn_conv_w[l], rnn_conv_b[l])
            hseq, h_new = rg_lru(xc, h_prev, rnn_gate_a_w[l], rnn_gate_a_b[l],
                                 rnn_gate_x_w[l], rnn_gate_x_b[l], rnn_lambda[l])
            rnn = hseq * jax.nn.gelu(rg)
            merged = (jax.nn.sigmoid(ga) * (attn @ w_branch[l, :D_ATTN])
                      + jax.nn.sigmoid(gb) * (rnn @ w_branch[l, D_ATTN:]))
            x = x + gate[:, None, :] * (merged @ w_out[l])
            shift, scale, gate = adaln(c, mod_ffn_w[l], mod_ffn_b[l])
            h = modulate(x, norm_ffn_g[l], shift, scale)
            up, cf_new = causal_dwconv(h @ ffn_up_w[l], cf_prev, ffn_conv_w[l], ffn_conv_b[l])
            val, gt = jnp.split(up, 2, axis=-1)
            x = x + gate[:, None, :] * ((val * jax.nn.gelu(gt)) @ ffn_down_w[l])
            ks_.append(k_new); vs_.append(v_new); crs.append(cr_new); hs_.append(h_new); cfs.append(cf_new)
        y = rmsnorm(x, final_norm_g)
        return y, jnp.stack(ks_), jnp.stack(vs_), jnp.stack(crs), jnp.stack(hs_), jnp.stack(cfs)

    y_prompt, k_p, v_p, rc_p, h_p, fc_p = run(x_prompt, c_prompt, None, None, None, None, None)
    y_sample, k_s, v_s, rc_s, h_s, fc_s = run(x_sample, c_sample, cache_k, cache_v,
                                              state_rnn_conv, state_rnn_h, state_ffn_conv)
    return (y_prompt, y_sample, k_p, v_p, k_s, v_s, rc_p, rc_s, h_p, h_s, fc_p, fc_s)
```

```python
import functools

import jax
import jax.numpy as jnp
import numpy as np
from jax import lax
from jax.experimental import pallas as pl
from jax.experimental.pallas import tpu as pltpu

F32 = jnp.float32
BF16 = jnp.bfloat16

D_MODEL = 1024
DEPTH = 2
CHUNK = 64
N_BAND_PAST = 8
BAND_PAST = N_BAND_PAST * CHUNK
BAND = BAND_PAST + CHUNK
N_HEADS = 8
HEAD_DIM = 64
D_ATTN = N_HEADS * HEAD_DIM
MAX_REL = 256
D_RNN = 512
N_RNN_BLOCKS = 8
CONV_W = 4
RG_C = 8.0
D_FF = 2816
FFN_CONV_W = 3
EPS = 1e-6
NEG_INF = -1e30

SUBLANES = 8
LANES = 128
HEADS_PER_VREG = LANES // HEAD_DIM
N_HEAD_PAIRS = N_HEADS // HEADS_PER_VREG
FFN_COLS = 256
VMEM_LIMIT_BYTES = 56 * 1024 * 1024

C_Q, C_K, C_V = 0, D_ATTN, 2 * D_ATTN
C_RX = 3 * D_ATTN
C_RG = C_RX + D_RNN
C_GA = C_RG + D_RNN
C_GB = C_GA + D_MODEL
D_IN = C_GB + D_MODEL


def _dot(a, b):
    return jnp.dot(a, b, preferred_element_type=F32)


def _modulated_norm(x, g, shift, scale):
    ms = jnp.mean(x * x, axis=-1, keepdims=True)
    y = x * lax.rsqrt(ms + EPS)
    return (y * g) * (1.0 + scale) + shift


def _mod_kernel(c_ref, w_ref, b_ref, o_ref):
    c = c_ref[...]
    s = (c * jax.nn.sigmoid(c)).astype(BF16)
    o_ref[...] = _dot(s, w_ref[...].astype(BF16)) + b_ref[...]


def _modulation(c_all, w, b):
    rows = c_all.shape[0]
    n_col = (3 * D_MODEL) // D_MODEL
    out = pl.pallas_call(
        _mod_kernel,
        grid=(DEPTH, n_col),
        in_specs=[
            pl.BlockSpec((rows, D_MODEL), lambda l, j: (0, 0)),
            pl.BlockSpec((None, D_MODEL, D_MODEL), lambda l, j: (l, 0, j)),
            pl.BlockSpec((None, 1, D_MODEL), lambda l, j: (l, 0, j)),
        ],
        out_specs=pl.BlockSpec((None, rows, D_MODEL), lambda l, j: (l, 0, j)),
        out_shape=jax.ShapeDtypeStruct((DEPTH, rows, 3 * D_MODEL), F32),
        compiler_params=pltpu.CompilerParams(dimension_semantics=("parallel", "parallel")),
        name="adaln_mod",
    )(c_all, w, b.reshape(DEPTH, 1, 3 * D_MODEL))
    return out.reshape(DEPTH, rows, 3, D_MODEL)


def _band_attention(q, kbuf, vbuf, bias_ref, row0, first_chunk, mask_past):
    lane = lax.broadcasted_iota(jnp.int32, (CHUNK, LANES), 1)
    low = lane < HEAD_DIM
    outs = []
    for p in range(N_HEAD_PAIRS):
        cols = slice(LANES * p, LANES * (p + 1))
        qp = q[:, cols]
        qs = jnp.concatenate(
            [jnp.where(low, qp, 0.0).astype(BF16), jnp.where(low, 0.0, qp).astype(BF16)], axis=0)
        kp = kbuf[row0:row0 + BAND, cols]
        s = lax.dot_general(qs, kp, (((1,), (1,)), ((), ())), preferred_element_type=F32)
        s = s + bias_ref[p]
        if mask_past:
            col = lax.broadcasted_iota(jnp.int32, (HEADS_PER_VREG * CHUNK, BAND), 1)
            s = jnp.where(col >= BAND_PAST - CHUNK * first_chunk, s, NEG_INF)
        m = jnp.max(s, axis=-1, keepdims=True)
        e = jnp.exp(s - m)
        denom = jnp.sum(e, axis=-1, keepdims=True)
        vp = vbuf[row0:row0 + BAND, cols]
        o = _dot(e.astype(BF16), vp) / denom
        outs.append(jnp.where(low, o[:CHUNK], o[CHUNK:]))
    return jnp.concatenate(outs, axis=1)


def _rg_lru_scan(a, u, h_prev):
    tm = a.shape[0]
    nv = tm // SUBLANES
    a3 = a.reshape(nv, SUBLANES, D_RNN)
    u3 = u.reshape(nv, SUBLANES, D_RNN)
    sub = lax.broadcasted_iota(jnp.int32, (nv, SUBLANES, D_RNN), 1)
    d = 1
    while d < SUBLANES:
        keep = sub >= d
        a_s = pltpu.roll(a3, d, axis=1)
        u_s = pltpu.roll(u3, d, axis=1)
        u3 = jnp.where(keep, a3 * u_s + u3, u3)
        a3 = jnp.where(keep, a3 * a_s, a3)
        d *= 2
    carry = h_prev
    hs = []
    for i in range(nv):
        hi = u3[i] + a3[i] * carry
        hs.append(hi)
        carry = hi[SUBLANES - 1:SUBLANES, :]
    return jnp.concatenate(hs, axis=0), carry


def _mixer_kernel(tm, n_t, has_past, keep_rows, *refs):
    refs = list(refs)
    (x_ref, mod_ref, g_ref, win_ref, bias_ref, cw_ref, cb_ref, gw_ref, gb_ref, lam_ref,
     wbr_ref, wout_ref) = refs[:12]
    refs = refs[12:]
    if has_past:
        pk_ref, pv_ref, cr0_ref, h0_ref = refs[:4]
        refs = refs[4:]
    xo_ref, ko_ref, vo_ref, cro_ref, ho_ref, kbuf, vbuf, rxbuf, hstate = refs

    t = pl.program_id(1)
    n_chunks = tm // CHUNK
    tail = CONV_W - 1

    @pl.when(t == 0)
    def _init():
        if has_past:
            kbuf[0:BAND_PAST, :] = pk_ref[...].astype(BF16)
            vbuf[0:BAND_PAST, :] = pv_ref[...].astype(BF16)
            rxbuf[SUBLANES - tail:SUBLANES, :] = cr0_ref[...]
            hstate[...] = h0_ref[...]
        else:
            kbuf[0:BAND_PAST, :] = jnp.zeros((BAND_PAST, D_ATTN), BF16)
            vbuf[0:BAND_PAST, :] = jnp.zeros((BAND_PAST, D_ATTN), BF16)
            rxbuf[0:SUBLANES, :] = jnp.zeros((SUBLANES, D_RNN), F32)
            hstate[...] = jnp.zeros((1, D_RNN), F32)

    x = x_ref[...]
    shift, scale, gate = mod_ref[0:1, :], mod_ref[1:2, :], mod_ref[2:3, :]
    hb = _modulated_norm(x, g_ref[...], shift, scale).astype(BF16)

    q = _dot(hb, win_ref[:, C_Q:C_Q + D_ATTN]) * (HEAD_DIM ** -0.5)
    k = _dot(hb, win_ref[:, C_K:C_K + D_ATTN])
    v = _dot(hb, win_ref[:, C_V:C_V + D_ATTN])
    kbuf[BAND_PAST:BAND_PAST + tm, :] = k.astype(BF16)
    vbuf[BAND_PAST:BAND_PAST + tm, :] = v.astype(BF16)

    first_keep = n_t - keep_rows // tm

    @pl.when(t >= first_keep)
    def _keep_kv():
        off = pl.multiple_of((t - first_keep) * tm, tm)
        ko_ref[pl.ds(off, tm), :] = k
        vo_ref[pl.ds(off, tm), :] = v

    attn = []
    for j in range(n_chunks):
        attn.append(_band_attention(q[j * CHUNK:(j + 1) * CHUNK, :], kbuf, vbuf, bias_ref,
                                    j * CHUNK, t * n_chunks + j, not has_past))
    attn = jnp.concatenate(attn, axis=0)

    for i in range(BAND_PAST // CHUNK):
        kbuf[i * CHUNK:(i + 1) * CHUNK, :] = kbuf[tm + i * CHUNK:tm + (i + 1) * CHUNK, :]
        vbuf[i * CHUNK:(i + 1) * CHUNK, :] = vbuf[tm + i * CHUNK:tm + (i + 1) * CHUNK, :]

    rx = _dot(hb, win_ref[:, C_RX:C_RX + D_RNN])
    rg = _dot(hb, win_ref[:, C_RG:C_RG + D_RNN])
    rxbuf[SUBLANES:SUBLANES + tm, :] = rx
    xc = cb_ref[...] + (cw_ref[0:1, :] * rxbuf[SUBLANES - 3:SUBLANES - 3 + tm, :]
                        + cw_ref[1:2, :] * rxbuf[SUBLANES - 2:SUBLANES - 2 + tm, :]
                        + cw_ref[2:3, :] * rxbuf[SUBLANES - 1:SUBLANES - 1 + tm, :]
                        + cw_ref[3:4, :] * rx)

    @pl.when(t == n_t - 1)
    def _keep_conv():
        cro_ref[...] = rxbuf[SUBLANES + tm - tail:SUBLANES + tm, :]

    rxbuf[0:SUBLANES, :] = rxbuf[tm:tm + SUBLANES, :]

    gates = _dot(xc.astype(BF16), gw_ref[...]) + gb_ref[...]
    r = jax.nn.sigmoid(gates[:, :D_RNN])
    ig = jax.nn.sigmoid(gates[:, D_RNN:])
    log_a = -RG_C * r * jax.nn.softplus(-lam_ref[...])
    a = jnp.exp(log_a)
    u = jnp.sqrt(-jnp.tanh(log_a) * (a * a + 1.0)) * (ig * xc)
    hseq, h_last = _rg_lru_scan(a, u, hstate[...])
    hstate[...] = h_last

    @pl.when(t == n_t - 1)
    def _keep_h():
        ho_ref[...] = h_last

    rnn = hseq * jax.nn.gelu(rg)

    ga = _dot(hb, win_ref[:, C_GA:C_GA + D_MODEL])
    gb = _dot(hb, win_ref[:, C_GB:C_GB + D_MODEL])
    merged = (jax.nn.sigmoid(ga) * _dot(attn.astype(BF16), wbr_ref[0:D_ATTN, :])
              + jax.nn.sigmoid(gb) * _dot(rnn.astype(BF16), wbr_ref[D_ATTN:, :]))
    xo_ref[...] = x + gate * _dot(merged.astype(BF16), wout_ref[...])


def _mixer_layer(l, x, mod, row0, norm_g, w_in, bias, conv_w, conv_b, gate_w, gate_b, lam,
                 w_branch, w_out, past, tm):
    n_b, n_tok, _ = x.shape
    n_t = n_tok // tm
    keep_rows = min(BAND_PAST, n_tok)
    assert n_tok % tm == 0 and tm % CHUNK == 0 and keep_rows % tm == 0
    has_past = past is not None
    tail = CONV_W - 1

    def whole(shape):
        return pl.BlockSpec(shape, lambda b, t: (0,) * len(shape), pipeline_mode=pl.Buffered(1))

    def layer(shape):
        return pl.BlockSpec((None,) + shape, lambda b, t: (l,) + (0,) * len(shape),
                            pipeline_mode=pl.Buffered(1))

    def per_seq(shape):
        return pl.BlockSpec((None,) + shape, lambda b, t: (b,) + (0,) * len(shape))

    def layer_seq(shape):
        return pl.BlockSpec((None, None) + shape, lambda b, t: (l, b) + (0,) * len(shape))

    in_specs = [
        pl.BlockSpec((None, tm, D_MODEL), lambda b, t: (b, t, 0)),
        pl.BlockSpec((None, None, 3, D_MODEL), lambda b, t: (l, b + row0, 0, 0)),
        layer((1, D_MODEL)),
        layer((D_MODEL, D_IN)),
        whole((N_HEAD_PAIRS, HEADS_PER_VREG * CHUNK, BAND)),
        layer((CONV_W, D_RNN)),
        layer((1, D_RNN)),
        layer((D_RNN, 2 * D_RNN)),
        layer((1, 2 * D_RNN)),
        layer((1, D_RNN)),
        layer((D_ATTN + D_RNN, D_MODEL)),
        layer((D_MODEL, D_MODEL)),
    ]
    args = [x, mod, norm_g, w_in, bias, conv_w, conv_b, gate_w, gate_b, lam, w_branch, w_out]
    if has_past:
        in_specs += [layer_seq((BAND_PAST, D_ATTN)), layer_seq((BAND_PAST, D_ATTN)),
                     layer_seq((tail, D_RNN)), layer_seq((1, D_RNN))]
        args += list(past)
    out_specs = [
        pl.BlockSpec((None, tm, D_MODEL), lambda b, t: (b, t, 0)),
        per_seq((keep_rows, D_ATTN)),
        per_seq((keep_rows, D_ATTN)),
        per_seq((tail, D_RNN)),
        per_seq((1, D_RNN)),
    ]
    out_shape = [
        jax.ShapeDtypeStruct((n_b, n_tok, D_MODEL), F32),
        jax.ShapeDtypeStruct((n_b, keep_rows, D_ATTN), F32),
        jax.ShapeDtypeStruct((n_b, keep_rows, D_ATTN), F32),
        jax.ShapeDtypeStruct((n_b, tail, D_RNN), F32),
        jax.ShapeDtypeStruct((n_b, 1, D_RNN), F32),
    ]
    scratch = [
        pltpu.VMEM((BAND_PAST + tm, D_ATTN), BF16),
        pltpu.VMEM((BAND_PAST + tm, D_ATTN), BF16),
        pltpu.VMEM((SUBLANES + tm, D_RNN), F32),
        pltpu.VMEM((1, D_RNN), F32),
    ]
    return pl.pallas_call(
        functools.partial(_mixer_kernel, tm, n_t, has_past, keep_rows),
        grid=(n_b, n_t),
        in_specs=in_specs,
        out_specs=out_specs,
        out_shape=out_shape,
        scratch_shapes=scratch,
        compiler_params=pltpu.CompilerParams(
            dimension_semantics=("parallel", "arbitrary"), vmem_limit_bytes=VMEM_LIMIT_BYTES),
        name=f"mixer_l{l}_{'sample' if has_past else 'prompt'}",
    )(*args)


def _ffn_kernel(tm, n_t, has_past, final_norm, *refs):
    refs = list(refs)
    x_ref, mod_ref, g_ref, wup_ref, cw_ref, cb_ref, wdn_ref = refs[:7]
    refs = refs[7:]
    if final_norm:
        fg_ref = refs[0]
        refs = refs[1:]
    if has_past:
        cf0_ref = refs[0]
        refs = refs[1:]
    xo_ref, cfo_ref, upbuf = refs

    t = pl.program_id(1)
    tail = FFN_CONV_W - 1

    @pl.when(t == 0)
    def _init():
        upbuf[0:SUBLANES, :] = jnp.zeros((SUBLANES, 2 * D_FF), F32)
        if has_past:
            upbuf[SUBLANES - tail:SUBLANES, :] = cf0_ref[...]

    x = x_ref[...]
    shift, scale, gate = mod_ref[0:1, :], mod_ref[1:2, :], mod_ref[2:3, :]
    hb = _modulated_norm(x, g_ref[...], shift, scale).astype(BF16)

    def conv_cols(c0):
        cols = slice(c0, c0 + FFN_COLS)
        up = _dot(hb, wup_ref[:, cols])
        upbuf[SUBLANES:SUBLANES + tm, cols] = up
        return cb_ref[:, cols] + (cw_ref[0:1, cols] * upbuf[SUBLANES - 2:SUBLANES - 2 + tm, cols]
                                  + cw_ref[1:2, cols] * upbuf[SUBLANES - 1:SUBLANES - 1 + tm, cols]
                                  + cw_ref[2:3, cols] * up)

    acc = jnp.zeros((tm, D_MODEL), F32)
    for c in range(D_FF // FFN_COLS):
        val = conv_cols(c * FFN_COLS)
        gt = conv_cols(D_FF + c * FFN_COLS)
        act = (val * jax.nn.gelu(gt)).astype(BF16)
        acc = acc + _dot(act, wdn_ref[c * FFN_COLS:(c + 1) * FFN_COLS, :])

    @pl.when(t == n_t - 1)
    def _keep_conv():
        cfo_ref[...] = upbuf[SUBLANES + tm - tail:SUBLANES + tm, :]

    upbuf[0:SUBLANES, :] = upbuf[tm:tm + SUBLANES, :]

    y = x + gate * acc
    if final_norm:
        ms = jnp.mean(y * y, axis=-1, keepdims=True)
        y = y * lax.rsqrt(ms + EPS) * fg_ref[...]
    xo_ref[...] = y


def _ffn_layer(l, x, mod, row0, norm_g, w_up, conv_w, conv_b, w_down, final_g, past, tm):
    n_b, n_tok, _ = x.shape
    n_t = n_tok // tm
    assert n_tok % tm == 0 and D_FF % FFN_COLS == 0
    has_past = past is not None
    final_norm = final_g is not None
    tail = FFN_CONV_W - 1

    def layer(shape):
        return pl.BlockSpec((None,) + shape, lambda b, t: (l,) + (0,) * len(shape),
                            pipeline_mode=pl.Buffered(1))

    in_specs = [
        pl.BlockSpec((None, tm, D_MODEL), lambda b, t: (b, t, 0)),
        pl.BlockSpec((None, None, 3, D_MODEL), lambda b, t: (l, b + row0, 0, 0)),
        layer((1, D_MODEL)),
        layer((D_MODEL, 2 * D_FF)),
        layer((FFN_CONV_W, 2 * D_FF)),
        layer((1, 2 * D_FF)),
        layer((D_FF, D_MODEL)),
    ]
    args = [x, mod, norm_g, w_up, conv_w, conv_b, w_down]
    if final_norm:
        in_specs.append(pl.BlockSpec((1, D_MODEL), lambda b, t: (0, 0)))
        args.append(final_g)
    if has_past:
        in_specs.append(pl.BlockSpec((None, None, tail, 2 * D_FF), lambda b, t: (l, b, 0, 0)))
        args.append(past)
    return pl.pallas_call(
        functools.partial(_ffn_kernel, tm, n_t, has_past, final_norm),
        grid=(n_b, n_t),
        in_specs=in_specs,
        out_specs=[
            pl.BlockSpec((None, tm, D_MODEL), lambda b, t: (b, t, 0)),
            pl.BlockSpec((None, tail, 2 * D_FF), lambda b, t: (b, 0, 0)),
        ],
        out_shape=[
            jax.ShapeDtypeStruct((n_b, n_tok, D_MODEL), F32),
            jax.ShapeDtypeStruct((n_b, tail, 2 * D_FF), F32),
        ],
        scratch_shapes=[pltpu.VMEM((SUBLANES + tm, 2 * D_FF), F32)],
        compiler_params=pltpu.CompilerParams(
            dimension_semantics=("parallel", "arbitrary"), vmem_limit_bytes=VMEM_LIMIT_BYTES),
        name=f"ffn_l{l}_{'sample' if has_past else 'prompt'}",
    )(*args)


def _pair_bias(table):
    rel = BAND_PAST + np.arange(CHUNK)[:, None] - np.arange(BAND)[None, :]
    idx = np.clip(rel, -MAX_REL, MAX_REL) + MAX_REL
    bias = jnp.transpose(table[idx], (2, 0, 1))
    return bias.reshape(N_HEAD_PAIRS, HEADS_PER_VREG * CHUNK, BAND)


def _block_diag(w):
    n, i, j = w.shape
    eye = jnp.eye(n, dtype=w.dtype)
    return (w[:, :, None, :] * eye[:, None, :, None]).reshape(n * i, n * j)


def kernel(x_prompt, x_sample, c_prompt, c_sample, cache_k, cache_v, state_rnn_conv, state_rnn_h,
           state_ffn_conv, mod_mix_w, mod_mix_b, norm_mix_g, w_in, rel_bias_table, rnn_conv_w,
           rnn_conv_b, rnn_gate_a_w, rnn_gate_a_b, rnn_gate_x_w, rnn_gate_x_b, rnn_lambda, w_branch,
           w_out, mod_ffn_w, mod_ffn_b, norm_ffn_g, ffn_up_w, ffn_conv_w, ffn_conv_b, ffn_down_w,
           final_norm_g):
    n_prompt = x_prompt.shape[0]
    n_sample = x_sample.shape[0]

    rows = n_prompt + n_sample
    pad = (-rows) % SUBLANES
    c_all = jnp.concatenate([c_prompt, c_sample, jnp.zeros((pad, D_MODEL), F32)], axis=0)
    mod_mix = _modulation(c_all, mod_mix_w, mod_mix_b)
    mod_ffn = _modulation(c_all, mod_ffn_w, mod_ffn_b)

    w_in_b = w_in.astype(BF16)
    w_branch_b = w_branch.astype(BF16)
    w_out_b = w_out.astype(BF16)
    w_up_b = ffn_up_w.astype(BF16)
    w_down_b = ffn_down_w.astype(BF16)
    gate_w = jnp.stack([
        jnp.concatenate([_block_diag(rnn_gate_a_w[l]), _block_diag(rnn_gate_x_w[l])], axis=1)
        for l in range(DEPTH)]).astype(BF16)
    gate_b = jnp.concatenate([rnn_gate_a_b, rnn_gate_x_b], axis=-1).reshape(DEPTH, 1, 2 * D_RNN)
    norm_mix = norm_mix_g.reshape(DEPTH, 1, D_MODEL)
    norm_ffn = norm_ffn_g.reshape(DEPTH, 1, D_MODEL)
    conv_b = rnn_conv_b.reshape(DEPTH, 1, D_RNN)
    lam = rnn_lambda.reshape(DEPTH, 1, D_RNN)
    fconv_b = ffn_conv_b.reshape(DEPTH, 1, 2 * D_FF)
    final_g = final_norm_g.reshape(1, D_MODEL)

    past_k = cache_k.reshape(DEPTH, n_sample, BAND_PAST, D_ATTN)
    past_v = cache_v.reshape(DEPTH, n_sample, BAND_PAST, D_ATTN)
    past_h = state_rnn_h.reshape(DEPTH, n_sample, 1, D_RNN)

    def run(x, row0, with_past, tm):
        ks, vs, crs, hs, cfs = [], [], [], [], []
        for l in range(DEPTH):
            bias = _pair_bias(rel_bias_table[l])
            past = (past_k, past_v, state_rnn_conv, past_h) if with_past else None
            x, k_new, v_new, cr_new, h_new = _mixer_layer(
                l, x, mod_mix, row0, norm_mix, w_in_b, bias, rnn_conv_w, conv_b, gate_w, gate_b, lam,
                w_branch_b, w_out_b, past, tm)
            x, cf_new = _ffn_layer(
                l, x, mod_ffn, row0, norm_ffn, w_up_b, ffn_conv_w, fconv_b, w_down_b,
                final_g if l == DEPTH - 1 else None, state_ffn_conv if with_past else None, tm)
            ks.append(k_new.reshape(k_new.shape[0], k_new.shape[1], N_HEADS, HEAD_DIM))
            vs.append(v_new.reshape(v_new.shape[0], v_new.shape[1], N_HEADS, HEAD_DIM))
            crs.append(cr_new)
            hs.append(h_new[:, 0, :])
            cfs.append(cf_new)
        return x, jnp.stack(ks), jnp.stack(vs), jnp.stack(crs), jnp.stack(hs), jnp.stack(cfs)

    y_p, k_p, v_p, rc_p, h_p, fc_p = run(x_prompt, 0, False, 256)
    y_s, k_s, v_s, rc_s, h_s, fc_s = run(x_sample, n_prompt, True, CHUNK)
    return (y_p, y_s, k_p, v_p, k_s, v_s, rc_p, rc_s, h_p, h_s, fc_p, fc_s)
```

```python
import functools

import jax
import jax.numpy as jnp
import numpy as np
from jax import lax
from jax.experimental import pallas as pl
from jax.experimental.pallas import tpu as pltpu

F32 = jnp.float32
BF16 = jnp.bfloat16

D_MODEL = 1024
DEPTH = 2
CHUNK = 64
N_BAND_PAST = 8
BAND_PAST = N_BAND_PAST * CHUNK
BAND = BAND_PAST + CHUNK
N_HEADS = 8
HEAD_DIM = 64
D_ATTN = N_HEADS * HEAD_DIM
MAX_REL = 256
D_RNN = 512
N_RNN_BLOCKS = 8
CONV_W = 4
RG_C = 8.0
D_FF = 2816
FFN_CONV_W = 3
EPS = 1e-6
NEG_INF = -1e30

SUBLANES = 8
LANES = 128
HEADS_PER_VREG = LANES // HEAD_DIM
N_HEAD_PAIRS = N_HEADS // HEADS_PER_VREG
PAIR_ROWS = HEADS_PER_VREG * CHUNK
BIAS_W = BAND + CHUNK
FFN_COLS = 256
GATE_COLS = 256
SCORES_AHEAD = 3
VMEM_LIMIT_BYTES = 56 * 1024 * 1024

C_Q, C_K, C_V = 0, D_ATTN, 2 * D_ATTN
C_RX = 3 * D_ATTN
C_RG = C_RX + D_RNN
C_GA = C_RG + D_RNN
C_GB = C_GA + D_MODEL
D_IN = C_GB + D_MODEL


def _dot(a, b):
    return jnp.dot(a, b, preferred_element_type=F32)


def _modulated_norm(x, g, shift, scale):
    ms = jnp.mean(x * x, axis=-1, keepdims=True)
    y = x * lax.rsqrt(ms + EPS)
    return (y * g) * (1.0 + scale) + shift


def _mod_kernel(c_ref, w_ref, b_ref, o_ref):
    c = c_ref[...]
    s = (c * jax.nn.sigmoid(c)).astype(BF16)
    o_ref[...] = _dot(s, w_ref[...].astype(BF16)) + b_ref[...]


def _modulation(c_all, w, b):
    rows = c_all.shape[0]
    n_col = (3 * D_MODEL) // D_MODEL
    out = pl.pallas_call(
        _mod_kernel,
        grid=(DEPTH, n_col),
        in_specs=[
            pl.BlockSpec((rows, D_MODEL), lambda l, j: (0, 0)),
            pl.BlockSpec((None, D_MODEL, D_MODEL), lambda l, j: (l, 0, j)),
            pl.BlockSpec((None, 1, D_MODEL), lambda l, j: (l, 0, j)),
        ],
        out_specs=pl.BlockSpec((None, rows, D_MODEL), lambda l, j: (l, 0, j)),
        out_shape=jax.ShapeDtypeStruct((DEPTH, rows, 3 * D_MODEL), F32),
        compiler_params=pltpu.CompilerParams(dimension_semantics=("parallel", "parallel")),
        name="adaln_mod",
    )(c_all, w, b.reshape(DEPTH, 1, 3 * D_MODEL))
    return out.reshape(DEPTH, rows, 3, D_MODEL)


def _expand_bias(g_ref, bias_buf):
    for p in range(N_HEAD_PAIRS):
        rows = []
        for hh in range(HEADS_PER_VREG):
            h = HEADS_PER_VREG * p + hh
            g = jnp.broadcast_to(g_ref[h:h + 1, :], (CHUNK, BIAS_W))
            rows.append(pltpu.roll(g, BAND, axis=1, stride=1, stride_axis=0)[:, :BAND])
        bias_buf[p] = jnp.concatenate(rows, axis=0)


def _scan_groups(a, u):
    nv = a.shape[0] // SUBLANES
    a3 = a.reshape(nv, SUBLANES, D_RNN)
    u3 = u.reshape(nv, SUBLANES, D_RNN)
    sub = lax.broadcasted_iota(jnp.int32, (nv, SUBLANES, D_RNN), 1)
    d = 1
    while d < SUBLANES:
        keep = sub >= d
        a_s = pltpu.roll(a3, d, axis=1)
        u_s = pltpu.roll(u3, d, axis=1)
        u3 = jnp.where(keep, a3 * u_s + u3, u3)
        a3 = jnp.where(keep, a3 * a_s, a3)
        d *= 2
    return a3, u3


def _mixer_kernel(tm, n_t, has_past, keep_rows, *refs):
    refs = list(refs)
    (x_ref, mod_ref, g_ref, win_ref, brow_ref, cw_ref, cb_ref, gw_ref, gb_ref, lam_ref,
     wbr_ref, wout_ref) = refs[:12]
    refs = refs[12:]
    if has_past:
        pk_ref, pv_ref, cr0_ref, h0_ref = refs[:4]
        refs = refs[4:]
    (xo_ref, ko_ref, vo_ref, cro_ref, ho_ref,
     kbuf, vbuf, rxbuf, hstate, bias_buf, attn_buf, mrg_buf, sga_buf) = refs

    t = pl.program_id(1)
    n_chunks = tm // CHUNK
    nv = tm // SUBLANES
    tail = CONV_W - 1

    @pl.when(t == 0)
    def _init():
        _expand_bias(brow_ref, bias_buf)
        if has_past:
            kbuf[0:BAND_PAST, :] = pk_ref[...].astype(BF16)
            vbuf[0:BAND_PAST, :] = pv_ref[...].astype(BF16)
            rxbuf[SUBLANES - tail:SUBLANES, :] = cr0_ref[...]
            hstate[...] = h0_ref[...]
        else:
            kbuf[0:BAND_PAST, :] = jnp.zeros((BAND_PAST, D_ATTN), BF16)
            vbuf[0:BAND_PAST, :] = jnp.zeros((BAND_PAST, D_ATTN), BF16)
            rxbuf[0:SUBLANES, :] = jnp.zeros((SUBLANES, D_RNN), F32)
            hstate[...] = jnp.zeros((1, D_RNN), F32)

    x = x_ref[...]
    shift, scale, gate = mod_ref[0:1, :], mod_ref[1:2, :], mod_ref[2:3, :]
    hb = _modulated_norm(x, g_ref[...], shift, scale).astype(BF16)

    k = _dot(hb, win_ref[:, C_K:C_K + D_ATTN])
    v = _dot(hb, win_ref[:, C_V:C_V + D_ATTN])
    kbuf[BAND_PAST:BAND_PAST + tm, :] = k.astype(BF16)
    vbuf[BAND_PAST:BAND_PAST + tm, :] = v.astype(BF16)
    first_keep = n_t - keep_rows // tm

    @pl.when(t >= first_keep)
    def _keep_kv():
        off = pl.multiple_of((t - first_keep) * tm, tm)
        ko_ref[pl.ds(off, tm), :] = k
        vo_ref[pl.ds(off, tm), :] = v

    q = _dot(hb, win_ref[:, C_Q:C_Q + D_ATTN]) * (HEAD_DIM ** -0.5)

    lane = lax.broadcasted_iota(jnp.int32, (CHUNK, LANES), 1)
    low = lane < HEAD_DIM
    units = [(j, p) for j in range(n_chunks) for p in range(N_HEAD_PAIRS)]

    def scores(j, p):
        cols = slice(LANES * p, LANES * (p + 1))
        qp = q[j * CHUNK:(j + 1) * CHUNK, cols]
        qs = jnp.concatenate(
            [jnp.where(low, qp, 0.0).astype(BF16), jnp.where(low, 0.0, qp).astype(BF16)], axis=0)
        kp = kbuf[j * CHUNK:j * CHUNK + BAND, cols]
        s = lax.dot_general(qs, kp, (((1,), (1,)), ((), ())), preferred_element_type=F32)
        s = s + bias_buf[p]
        if not has_past:
            col = lax.broadcasted_iota(jnp.int32, (PAIR_ROWS, BAND), 1)
            s = jnp.where(col >= BAND_PAST - CHUNK * (t * n_chunks + j), s, NEG_INF)
        return s

    def weighted(j, p, s):
        cols = slice(LANES * p, LANES * (p + 1))
        m = jnp.max(s, axis=-1, keepdims=True)
        e = jnp.exp(s - m)
        denom = jnp.sum(e, axis=-1, keepdims=True)
        o = _dot(e.astype(BF16), vbuf[j * CHUNK:j * CHUNK + BAND, cols]) / denom
        attn_buf[j * CHUNK:(j + 1) * CHUNK, cols] = jnp.where(low, o[:CHUNK], o[CHUNK:]).astype(BF16)

    st = {}

    def piece_rx():
        rx = _dot(hb, win_ref[:, C_RX:C_RX + D_RNN])
        rxbuf[SUBLANES:SUBLANES + tm, :] = rx
        st["xc"] = cb_ref[...] + (cw_ref[0:1, :] * rxbuf[SUBLANES - 3:SUBLANES - 3 + tm, :]
                                  + cw_ref[1:2, :] * rxbuf[SUBLANES - 2:SUBLANES - 2 + tm, :]
                                  + cw_ref[2:3, :] * rxbuf[SUBLANES - 1:SUBLANES - 1 + tm, :]
                                  + cw_ref[3:4, :] * rx)

        @pl.when(t == n_t - 1)
        def _keep_conv():
            cro_ref[...] = rxbuf[SUBLANES + tm - tail:SUBLANES + tm, :]

        rxbuf[0:SUBLANES, :] = rxbuf[tm:tm + SUBLANES, :]

    def piece_gates():
        st["gates"] = _dot(st["xc"].astype(BF16), gw_ref[...]) + gb_ref[...]
        st["carry"] = hstate[...]
        st["hs"] = []

    def piece_scan(blk):
        def run():
            rows = slice(blk * CHUNK, (blk + 1) * CHUNK)
            gates = st["gates"][rows, :]
            xc = st["xc"][rows, :]
            r = jax.nn.sigmoid(gates[:, :D_RNN])
            ig = jax.nn.sigmoid(gates[:, D_RNN:])
            log_a = -RG_C * r * jax.nn.softplus(-lam_ref[...])
            a = jnp.exp(log_a)
            u = jnp.sqrt(-jnp.tanh(log_a) * (a * a + 1.0)) * (ig * xc)
            a3, u3 = _scan_groups(a, u)
            carry = st["carry"]
            for i in range(CHUNK // SUBLANES):
                hi = u3[i] + a3[i] * carry
                st["hs"].append(hi)
                carry = hi[SUBLANES - 1:SUBLANES, :]
            st["carry"] = carry
        return run

    def piece_rnn():
        h_last = st["carry"]
        hstate[...] = h_last

        @pl.when(t == n_t - 1)
        def _keep_h():
            ho_ref[...] = h_last

        rg = _dot(hb, win_ref[:, C_RG:C_RG + D_RNN])
        st["rnn"] = (jnp.concatenate(st["hs"], axis=0) * jax.nn.gelu(rg)).astype(BF16)

    def piece_gate_b(c):
        def run():
            cols = slice(c * GATE_COLS, (c + 1) * GATE_COLS)
            gb = _dot(hb, win_ref[:, C_GB + c * GATE_COLS:C_GB + (c + 1) * GATE_COLS])
            mrg_buf[:, cols] = jax.nn.sigmoid(gb) * _dot(st["rnn"], wbr_ref[D_ATTN:, cols])
        return run

    def piece_gate_a(c):
        def run():
            cols = slice(c * GATE_COLS, (c + 1) * GATE_COLS)
            ga = _dot(hb, win_ref[:, C_GA + c * GATE_COLS:C_GA + (c + 1) * GATE_COLS])
            sga_buf[:, cols] = jax.nn.sigmoid(ga)
        return run

    n_gate = D_MODEL // GATE_COLS
    pieces = [piece_rx, piece_gates] + [piece_scan(b) for b in range(n_chunks)] + [piece_rnn]
    pieces += [piece_gate_b(c) for c in range(n_gate)] + [piece_gate_a(c) for c in range(n_gate)]

    pending = [scores(*units[i]) for i in range(min(SCORES_AHEAD, len(units)))]
    for i, (j, p) in enumerate(units):
        if pieces:
            pieces.pop(0)()
        weighted(j, p, pending.pop(0))
        if i + SCORES_AHEAD < len(units):
            pending.append(scores(*units[i + SCORES_AHEAD]))
    for piece in pieces:
        piece()

    for i in range(BAND_PAST // CHUNK):
        kbuf[i * CHUNK:(i + 1) * CHUNK, :] = kbuf[tm + i * CHUNK:tm + (i + 1) * CHUNK, :]
        vbuf[i * CHUNK:(i + 1) * CHUNK, :] = vbuf[tm + i * CHUNK:tm + (i + 1) * CHUNK, :]

    merged = []
    for c in range(n_gate):
        cols = slice(c * GATE_COLS, (c + 1) * GATE_COLS)
        ma = _dot(attn_buf[...], wbr_ref[0:D_ATTN, cols])
        merged.append((sga_buf[:, cols] * ma + mrg_buf[:, cols]).astype(BF16))
    merged = jnp.concatenate(merged, axis=1)
    xo_ref[...] = x + gate * _dot(merged, wout_ref[...])


def _mixer_layer(l, x, mod, row0, norm_g, w_in, bias_rows, conv_w, conv_b, gate_w, gate_b, lam,
                 w_branch, w_out, past, tm):
    n_b, n_tok, _ = x.shape
    n_t = n_tok // tm
    keep_rows = min(BAND_PAST, n_tok)
    assert n_tok % tm == 0 and tm % CHUNK == 0 and keep_rows % tm == 0
    has_past = past is not None
    tail = CONV_W - 1

    def layer(shape):
        return pl.BlockSpec((None,) + shape, lambda b, t: (l,) + (0,) * len(shape),
                            pipeline_mode=pl.Buffered(1))

    def per_seq(shape):
        return pl.BlockSpec((None,) + shape, lambda b, t: (b,) + (0,) * len(shape))

    def layer_seq(shape):
        return pl.BlockSpec((None, None) + shape, lambda b, t: (l, b) + (0,) * len(shape))

    in_specs = [
        pl.BlockSpec((None, tm, D_MODEL), lambda b, t: (b, t, 0)),
        pl.BlockSpec((None, None, 3, D_MODEL), lambda b, t: (l, b + row0, 0, 0)),
        layer((1, D_MODEL)),
        layer((D_MODEL, D_IN)),
        layer((N_HEADS, BIAS_W)),
        layer((CONV_W, D_RNN)),
        layer((1, D_RNN)),
        layer((D_RNN, 2 * D_RNN)),
        layer((1, 2 * D_RNN)),
        layer((1, D_RNN)),
        layer((D_ATTN + D_RNN, D_MODEL)),
        layer((D_MODEL, D_MODEL)),
    ]
    args = [x, mod, norm_g, w_in, bias_rows, conv_w, conv_b, gate_w, gate_b, lam, w_branch, w_out]
    if has_past:
        in_specs += [layer_seq((BAND_PAST, D_ATTN)), layer_seq((BAND_PAST, D_ATTN)),
                     layer_seq((tail, D_RNN)), layer_seq((1, D_RNN))]
        args += list(past)
    out_specs = [
        pl.BlockSpec((None, tm, D_MODEL), lambda b, t: (b, t, 0)),
        per_seq((keep_rows, D_ATTN)),
        per_seq((keep_rows, D_ATTN)),
        per_seq((tail, D_RNN)),
        per_seq((1, D_RNN)),
    ]
    out_shape = [
        jax.ShapeDtypeStruct((n_b, n_tok, D_MODEL), F32),
        jax.ShapeDtypeStruct((n_b, keep_rows, D_ATTN), F32),
        jax.ShapeDtypeStruct((n_b, keep_rows, D_ATTN), F32),
        jax.ShapeDtypeStruct((n_b, tail, D_RNN), F32),
        jax.ShapeDtypeStruct((n_b, 1, D_RNN), F32),
    ]
    scratch = [
        pltpu.VMEM((BAND_PAST + tm, D_ATTN), BF16),
        pltpu.VMEM((BAND_PAST + tm, D_ATTN), BF16),
        pltpu.VMEM((SUBLANES + tm, D_RNN), F32),
        pltpu.VMEM((1, D_RNN), F32),
        pltpu.VMEM((N_HEAD_PAIRS, PAIR_ROWS, BAND), F32),
        pltpu.VMEM((tm, D_ATTN), BF16),
        pltpu.VMEM((tm, D_MODEL), F32),
        pltpu.VMEM((tm, D_MODEL), F32),
    ]
    return pl.pallas_call(
        functools.partial(_mixer_kernel, tm, n_t, has_past, keep_rows),
        grid=(n_b, n_t),
        in_specs=in_specs,
        out_specs=out_specs,
        out_shape=out_shape,
        scratch_shapes=scratch,
        compiler_params=pltpu.CompilerParams(
            dimension_semantics=("parallel", "arbitrary"), vmem_limit_bytes=VMEM_LIMIT_BYTES),
        name=f"mixer_l{l}_{'sample' if has_past else 'prompt'}",
    )(*args)


def _ffn_kernel(tm, n_t, has_past, final_norm, *refs):
    refs = list(refs)
    x_ref, mod_ref, g_ref, wup_ref, cw_ref, cb_ref, wdn_ref = refs[:7]
    refs = refs[7:]
    if final_norm:
        fg_ref = refs[0]
        refs = refs[1:]
    if has_past:
        cf0_ref = refs[0]
        refs = refs[1:]
    xo_ref, cfo_ref, upbuf = refs

    t = pl.program_id(1)
    tail = FFN_CONV_W - 1

    @pl.when(t == 0)
    def _init():
        upbuf[0:SUBLANES, :] = jnp.zeros((SUBLANES, 2 * D_FF), F32)
        if has_past:
            upbuf[SUBLANES - tail:SUBLANES, :] = cf0_ref[...]

    x = x_ref[...]
    shift, scale, gate = mod_ref[0:1, :], mod_ref[1:2, :], mod_ref[2:3, :]
    hb = _modulated_norm(x, g_ref[...], shift, scale).astype(BF16)

    def project(c):
        for c0 in (c * FFN_COLS, D_FF + c * FFN_COLS):
            cols = slice(c0, c0 + FFN_COLS)
            upbuf[SUBLANES:SUBLANES + tm, cols] = _dot(hb, wup_ref[:, cols])

    def conv_cols(c0):
        cols = slice(c0, c0 + FFN_COLS)
        return cb_ref[:, cols] + (cw_ref[0:1, cols] * upbuf[SUBLANES - 2:SUBLANES - 2 + tm, cols]
                                  + cw_ref[1:2, cols] * upbuf[SUBLANES - 1:SUBLANES - 1 + tm, cols]
                                  + cw_ref[2:3, cols] * upbuf[SUBLANES:SUBLANES + tm, cols])

    n_chunks = D_FF // FFN_COLS
    acc = jnp.zeros((tm, D_MODEL), F32)
    project(0)
    for c in range(n_chunks):
        if c + 1 < n_chunks:
            project(c + 1)
        val = conv_cols(c * FFN_COLS)
        gt = conv_cols(D_FF + c * FFN_COLS)
        act = (val * jax.nn.gelu(gt)).astype(BF16)
        acc = acc + _dot(act, wdn_ref[c * FFN_COLS:(c + 1) * FFN_COLS, :])

    @pl.when(t == n_t - 1)
    def _keep_conv():
        cfo_ref[...] = upbuf[SUBLANES + tm - tail:SUBLANES + tm, :]

    upbuf[0:SUBLANES, :] = upbuf[tm:tm + SUBLANES, :]

    y = x + gate * acc
    if final_norm:
        ms = jnp.mean(y * y, axis=-1, keepdims=True)
        y = y * lax.rsqrt(ms + EPS) * fg_ref[...]
    xo_ref[...] = y


def _ffn_layer(l, x, mod, row0, norm_g, w_up, conv_w, conv_b, w_down, final_g, past, tm):
    n_b, n_tok, _ = x.shape
    n_t = n_tok // tm
    assert n_tok % tm == 0 and D_FF % FFN_COLS == 0
    has_past = past is not None
    final_norm = final_g is not None
    tail = FFN_CONV_W - 1

    def layer(shape):
        return pl.BlockSpec((None,) + shape, lambda b, t: (l,) + (0,) * len(shape),
                            pipeline_mode=pl.Buffered(1))

    in_specs = [
        pl.BlockSpec((None, tm, D_MODEL), lambda b, t: (b, t, 0)),
        pl.BlockSpec((None, None, 3, D_MODEL), lambda b, t: (l, b + row0, 0, 0)),
        layer((1, D_MODEL)),
        layer((D_MODEL, 2 * D_FF)),
        layer((FFN_CONV_W, 2 * D_FF)),
        layer((1, 2 * D_FF)),
        layer((D_FF, D_MODEL)),
    ]
    args = [x, mod, norm_g, w_up, conv_w, conv_b, w_down]
    if final_norm:
        in_specs.append(pl.BlockSpec((1, D_MODEL), lambda b, t: (0, 0)))
        args.append(final_g)
    if has_past:
        in_specs.append(pl.BlockSpec((None, None, tail, 2 * D_FF), lambda b, t: (l, b, 0, 0)))
        args.append(past)
    return pl.pallas_call(
        functools.partial(_ffn_kernel, tm, n_t, has_past, final_norm),
        grid=(n_b, n_t),
        in_specs=in_specs,
        out_specs=[
            pl.BlockSpec((None, tm, D_MODEL), lambda b, t: (b, t, 0)),
            pl.BlockSpec((None, tail, 2 * D_FF), lambda b, t: (b, 0, 0)),
        ],
        out_shape=[
            jax.ShapeDtypeStruct((n_b, n_tok, D_MODEL), F32),
            jax.ShapeDtypeStruct((n_b, tail, 2 * D_FF), F32),
        ],
        scratch_shapes=[pltpu.VMEM((SUBLANES + tm, 2 * D_FF), F32)],
        compiler_params=pltpu.CompilerParams(
            dimension_semantics=("parallel", "arbitrary"), vmem_limit_bytes=VMEM_LIMIT_BYTES),
        name=f"ffn_l{l}_{'sample' if has_past else 'prompt'}",
    )(*args)


def _bias_rows(table):
    idx = np.clip(BAND - np.arange(BIAS_W), -MAX_REL, MAX_REL) + MAX_REL
    return jnp.transpose(table[:, idx, :], (0, 2, 1))


def _block_diag(w):
    n, i, j = w.shape
    eye = jnp.eye(n, dtype=w.dtype)
    return (w[:, :, None, :] * eye[:, None, :, None]).reshape(n * i, n * j)


def kernel(x_prompt, x_sample, c_prompt, c_sample, cache_k, cache_v, state_rnn_conv, state_rnn_h,
           state_ffn_conv, mod_mix_w, mod_mix_b, norm_mix_g, w_in, rel_bias_table, rnn_conv_w,
           rnn_conv_b, rnn_gate_a_w, rnn_gate_a_b, rnn_gate_x_w, rnn_gate_x_b, rnn_lambda, w_branch,
           w_out, mod_ffn_w, mod_ffn_b, norm_ffn_g, ffn_up_w, ffn_conv_w, ffn_conv_b, ffn_down_w,
           final_norm_g):
    n_prompt = x_prompt.shape[0]
    n_sample = x_sample.shape[0]

    rows = n_prompt + n_sample
    pad = (-rows) % SUBLANES
    c_all = jnp.concatenate([c_prompt, c_sample, jnp.zeros((pad, D_MODEL), F32)], axis=0)
    mod_mix = _modulation(c_all, mod_mix_w, mod_mix_b)
    mod_ffn = _modulation(c_all, mod_ffn_w, mod_ffn_b)

    w_in_b = w_in.astype(BF16)
    w_branch_b = w_branch.astype(BF16)
    w_out_b = w_out.astype(BF16)
    w_up_b = ffn_up_w.astype(BF16)
    w_down_b = ffn_down_w.astype(BF16)
    gate_w = jnp.stack([
        jnp.concatenate([_block_diag(rnn_gate_a_w[l]), _block_diag(rnn_gate_x_w[l])], axis=1)
        for l in range(DEPTH)]).astype(BF16)
    gate_b = jnp.concatenate([rnn_gate_a_b, rnn_gate_x_b], axis=-1).reshape(DEPTH, 1, 2 * D_RNN)
    bias_rows = _bias_rows(rel_bias_table)
    norm_mix = norm_mix_g.reshape(DEPTH, 1, D_MODEL)
    norm_ffn = norm_ffn_g.reshape(DEPTH, 1, D_MODEL)
    conv_b = rnn_conv_b.reshape(DEPTH, 1, D_RNN)
    lam = rnn_lambda.reshape(DEPTH, 1, D_RNN)
    fconv_b = ffn_conv_b.reshape(DEPTH, 1, 2 * D_FF)
    final_g = final_norm_g.reshape(1, D_MODEL)

    past_k = cache_k.reshape(DEPTH, n_sample, BAND_PAST, D_ATTN)
    past_v = cache_v.reshape(DEPTH, n_sample, BAND_PAST, D_ATTN)
    past_h = state_rnn_h.reshape(DEPTH, n_sample, 1, D_RNN)

    def run(x, row0, with_past, tm):
        ks, vs, crs, hs, cfs = [], [], [], [], []
        for l in range(DEPTH):
            past = (past_k, past_v, state_rnn_conv, past_h) if with_past else None
            x, k_new, v_new, cr_new, h_new = _mixer_layer(
                l, x, mod_mix, row0, norm_mix, w_in_b, bias_rows, rnn_conv_w, conv_b, gate_w, gate_b,
                lam, w_branch_b, w_out_b, past, tm)
            x, cf_new = _ffn_layer(
                l, x, mod_ffn, row0, norm_ffn, w_up_b, ffn_conv_w, fconv_b, w_down_b,
                final_g if l == DEPTH - 1 else None, state_ffn_conv if with_past else None, tm)
            ks.append(k_new.reshape(k_new.shape[0], k_new.shape[1], N_HEADS, HEAD_DIM))
            vs.append(v_new.reshape(v_new.shape[0], v_new.shape[1], N_HEADS, HEAD_DIM))
            crs.append(cr_new)
            hs.append(h_new[:, 0, :])
            cfs.append(cf_new)
        return x, jnp.stack(ks), jnp.stack(vs), jnp.stack(crs), jnp.stack(hs), jnp.stack(cfs)

    y_p, k_p, v_p, rc_p, h_p, fc_p = run(x_prompt, 0, False, 256)
    y_s, k_s, v_s, rc_s, h_s, fc_s = run(x_sample, n_prompt, True, CHUNK)
    return (y_p, y_s, k_p, v_p, k_s, v_s, rc_p, rc_s, h_p, h_s, fc_p, fc_s)
```

```python
import functools

import jax
import jax.numpy as jnp
import numpy as np
from jax import lax
from jax.experimental import pallas as pl
from jax.experimental.pallas import tpu as pltpu

F32 = jnp.float32
BF16 = jnp.bfloat16

D_MODEL = 1024
DEPTH = 2
CHUNK = 64
N_BAND_PAST = 8
BAND_PAST = N_BAND_PAST * CHUNK
BAND = BAND_PAST + CHUNK
N_HEADS = 8
HEAD_DIM = 64
D_ATTN = N_HEADS * HEAD_DIM
MAX_REL = 256
D_RNN = 512
N_RNN_BLOCKS = 8
CONV_W = 4
RG_C = 8.0
D_FF = 2816
FFN_CONV_W = 3
EPS = 1e-6
NEG_INF = -1e30

SUBLANES = 8
LANES = 128
MXU_TILE = 256
HEADS_PER_VREG = LANES // HEAD_DIM
N_HEAD_PAIRS = N_HEADS // HEADS_PER_VREG
PAIR_ROWS = HEADS_PER_VREG * CHUNK
BIAS_W = BAND + CHUNK
FFN_COLS = MXU_TILE
GATE_COLS = MXU_TILE
RNN_GATE_BLK = MXU_TILE
SCORES_AHEAD = 3
TILE_ROWS = 256
VMEM_LIMIT_BYTES = 56 * 1024 * 1024

C_Q, C_K, C_V = 0, D_ATTN, 2 * D_ATTN
C_RX = 3 * D_ATTN
C_RG = C_RX + D_RNN
C_GA = C_RG + D_RNN
C_GB = C_GA + D_MODEL
D_IN = C_GB + D_MODEL


def _dot(a, b):
    return jnp.dot(a, b, preferred_element_type=F32)


def _modulated_norm(x, g, shift, scale):
    ms = jnp.mean(x * x, axis=-1, keepdims=True)
    y = x * lax.rsqrt(ms + EPS)
    return (y * g) * (1.0 + scale) + shift


def _per_row(mod_ref, r, n_seg, seg_len):
    if n_seg == 1:
        return mod_ref[0, r:r + 1, :]
    return jnp.concatenate(
        [jnp.broadcast_to(mod_ref[i, r:r + 1, :], (seg_len, D_MODEL)) for i in range(n_seg)], axis=0)


def _tile_plan(n_seq, n_tok, independent):
    if independent:
        n_seg = max(1, min(n_seq, TILE_ROWS // n_tok))
        assert n_seq % n_seg == 0 and n_tok == CHUNK
        return n_seg, 1
    assert n_tok % TILE_ROWS == 0
    return 1, n_tok // TILE_ROWS


def _mod_kernel(c_ref, w_ref, b_ref, o_ref):
    c = c_ref[...]
    s = (c * jax.nn.sigmoid(c)).astype(BF16)
    o_ref[...] = _dot(s, w_ref[...].astype(BF16)) + b_ref[...]


def _modulation(c_all, w, b):
    rows = c_all.shape[0]
    n_col = (3 * D_MODEL) // D_MODEL
    out = pl.pallas_call(
        _mod_kernel,
        grid=(DEPTH, n_col),
        in_specs=[
            pl.BlockSpec((rows, D_MODEL), lambda l, j: (0, 0)),
            pl.BlockSpec((None, D_MODEL, D_MODEL), lambda l, j: (l, 0, j)),
            pl.BlockSpec((None, 1, D_MODEL), lambda l, j: (l, 0, j)),
        ],
        out_specs=pl.BlockSpec((None, rows, D_MODEL), lambda l, j: (l, 0, j)),
        out_shape=jax.ShapeDtypeStruct((DEPTH, rows, 3 * D_MODEL), F32),
        compiler_params=pltpu.CompilerParams(dimension_semantics=("parallel", "parallel")),
        name="adaln_mod",
    )(c_all, w, b.reshape(DEPTH, 1, 3 * D_MODEL))
    return out.reshape(DEPTH, rows, 3, D_MODEL)


def _expand_bias(g_ref, bias_buf):
    for p in range(N_HEAD_PAIRS):
        rows = []
        for hh in range(HEADS_PER_VREG):
            h = HEADS_PER_VREG * p + hh
            g = jnp.broadcast_to(g_ref[h:h + 1, :], (CHUNK, BIAS_W))
            rows.append(pltpu.roll(g, BAND, axis=1, stride=1, stride_axis=0)[:, :BAND])
        bias_buf[p] = jnp.concatenate(rows, axis=0)


def _scan_groups(a, u):
    nv = a.shape[0] // SUBLANES
    a3 = a.reshape(nv, SUBLANES, D_RNN)
    u3 = u.reshape(nv, SUBLANES, D_RNN)
    sub = lax.broadcasted_iota(jnp.int32, (nv, SUBLANES, D_RNN), 1)
    d = 1
    while d < SUBLANES:
        keep = sub >= d
        a_s = pltpu.roll(a3, d, axis=1)
        u_s = pltpu.roll(u3, d, axis=1)
        u3 = jnp.where(keep, a3 * u_s + u3, u3)
        a3 = jnp.where(keep, a3 * a_s, a3)
        d *= 2
    return a3, u3


def _mixer_kernel(tm, n_t, n_seg, has_past, *refs):
    refs = list(refs)
    (x_ref, mod_ref, g_ref, win_ref, brow_ref, cw_ref, cb_ref, gw_ref, gb_ref, lam_ref,
     wbr_ref, wout_ref) = refs[:12]
    refs = refs[12:]
    if has_past:
        pk_ref, pv_ref, cr0_ref, h0_ref = refs[:4]
        refs = refs[4:]
    (xo_ref, ko_ref, vo_ref, cro_ref, ho_ref,
     kbuf, vbuf, rxbuf, hstate, bias_buf, attn_buf, mrg_buf, sga_buf) = refs

    t = pl.program_id(1)
    seg_len = tm // n_seg
    n_chunks = tm // CHUNK
    tail = CONV_W - 1
    rx_stride = SUBLANES + seg_len
    band_stride = BAND if has_past else CHUNK

    @pl.when(t == 0)
    def _init():
        _expand_bias(brow_ref, bias_buf)
        if has_past:
            for i in range(n_seg):
                kbuf[i * BAND:i * BAND + BAND_PAST, :] = pk_ref[i].astype(BF16)
                vbuf[i * BAND:i * BAND + BAND_PAST, :] = pv_ref[i].astype(BF16)
                rxbuf[i * rx_stride + SUBLANES - tail:i * rx_stride + SUBLANES, :] = cr0_ref[i]
        else:
            kbuf[0:BAND_PAST, :] = jnp.zeros((BAND_PAST, D_ATTN), BF16)
            vbuf[0:BAND_PAST, :] = jnp.zeros((BAND_PAST, D_ATTN), BF16)
            rxbuf[0:SUBLANES, :] = jnp.zeros((SUBLANES, D_RNN), F32)
            hstate[...] = jnp.zeros((1, D_RNN), F32)

    x = x_ref[...]
    shift, scale, gate = (_per_row(mod_ref, r, n_seg, seg_len) for r in range(3))
    hb = _modulated_norm(x, g_ref[...], shift, scale).astype(BF16)

    k = _dot(hb, win_ref[:, C_K:C_K + D_ATTN])
    v = _dot(hb, win_ref[:, C_V:C_V + D_ATTN])
    if has_past:
        for i in range(n_seg):
            rows = slice(i * CHUNK, (i + 1) * CHUNK)
            kbuf[i * BAND + BAND_PAST:(i + 1) * BAND, :] = k[rows].astype(BF16)
            vbuf[i * BAND + BAND_PAST:(i + 1) * BAND, :] = v[rows].astype(BF16)
            ko_ref[i] = k[rows]
            vo_ref[i] = v[rows]
    else:
        kbuf[BAND_PAST:BAND_PAST + tm, :] = k.astype(BF16)
        vbuf[BAND_PAST:BAND_PAST + tm, :] = v.astype(BF16)
        first_keep = n_t - BAND_PAST // tm

        @pl.when(t >= first_keep)
        def _keep_kv():
            off = pl.multiple_of((t - first_keep) * tm, tm)
            ko_ref[0, pl.ds(off, tm), :] = k
            vo_ref[0, pl.ds(off, tm), :] = v

    q = _dot(hb, win_ref[:, C_Q:C_Q + D_ATTN]) * (HEAD_DIM ** -0.5)

    lane = lax.broadcasted_iota(jnp.int32, (CHUNK, LANES), 1)
    low = lane < HEAD_DIM
    units = [(j, p) for j in range(n_chunks) for p in range(N_HEAD_PAIRS)]

    def scores(j, p):
        cols = slice(LANES * p, LANES * (p + 1))
        qp = q[j * CHUNK:(j + 1) * CHUNK, cols]
        qs = jnp.concatenate(
            [jnp.where(low, qp, 0.0).astype(BF16), jnp.where(low, 0.0, qp).astype(BF16)], axis=0)
        kp = kbuf[j * band_stride:j * band_stride + BAND, cols]
        s = lax.dot_general(qs, kp, (((1,), (1,)), ((), ())), preferred_element_type=F32)
        s = s + bias_buf[p]
        if not has_past:
            col = lax.broadcasted_iota(jnp.int32, (PAIR_ROWS, BAND), 1)
            s = jnp.where(col >= BAND_PAST - CHUNK * (t * n_chunks + j), s, NEG_INF)
        return s

    def probs(s):
        m = jnp.max(s, axis=-1, keepdims=True)
        e = jnp.exp(s - m)
        return e.astype(BF16), jnp.sum(e, axis=-1, keepdims=True)

    def weighted(j, p, e, denom):
        cols = slice(LANES * p, LANES * (p + 1))
        o = _dot(e, vbuf[j * band_stride:j * band_stride + BAND, cols]) / denom
        attn_buf[j * CHUNK:(j + 1) * CHUNK, cols] = jnp.where(low, o[:CHUNK], o[CHUNK:]).astype(BF16)

    st = {}

    def piece_rx():
        rx = _dot(hb, win_ref[:, C_RX:C_RX + D_RNN])
        xc = []
        for i in range(n_seg):
            base = i * rx_stride + SUBLANES
            rxs = rx[i * seg_len:(i + 1) * seg_len]
            rxbuf[base:base + seg_len, :] = rxs
            xc.append(cb_ref[...] + (cw_ref[0:1, :] * rxbuf[base - 3:base - 3 + seg_len, :]
                                     + cw_ref[1:2, :] * rxbuf[base - 2:base - 2 + seg_len, :]
                                     + cw_ref[2:3, :] * rxbuf[base - 1:base - 1 + seg_len, :]
                                     + cw_ref[3:4, :] * rxs))
            if has_past:
                cro_ref[i] = rxbuf[base + seg_len - tail:base + seg_len, :]
        st["xc"] = jnp.concatenate(xc, axis=0)
        if not has_past:
            @pl.when(t == n_t - 1)
            def _keep_conv():
                cro_ref[0] = rxbuf[SUBLANES + tm - tail:SUBLANES + tm, :]

            rxbuf[0:SUBLANES, :] = rxbuf[tm:tm + SUBLANES, :]

    def piece_gates():
        xcb = st["xc"].astype(BF16)
        n_blk = D_RNN // RNN_GATE_BLK
        gates = [_dot(xcb[:, (i % n_blk) * RNN_GATE_BLK:(i % n_blk + 1) * RNN_GATE_BLK], gw_ref[i])
                 for i in range(2 * n_blk)]
        st["gates"] = jnp.concatenate(gates, axis=1) + gb_ref[...]
        st["carry"] = None if has_past else hstate[...]
        st["hs"] = []

    def piece_scan(blk):
        def run():
            rows = slice(blk * CHUNK, (blk + 1) * CHUNK)
            gates = st["gates"][rows, :]
            xc = st["xc"][rows, :]
            r = jax.nn.sigmoid(gates[:, :D_RNN])
            ig = jax.nn.sigmoid(gates[:, D_RNN:])
            log_a = -RG_C * r * jax.nn.softplus(-lam_ref[...])
            a = jnp.exp(log_a)
            u = jnp.sqrt(-jnp.tanh(log_a) * (a * a + 1.0)) * (ig * xc)
            a3, u3 = _scan_groups(a, u)
            carry = h0_ref[blk] if has_past else st["carry"]
            for i in range(CHUNK // SUBLANES):
                hi = u3[i] + a3[i] * carry
                st["hs"].append(hi)
                carry = hi[SUBLANES - 1:SUBLANES, :]
            if has_past:
                ho_ref[blk] = carry
            else:
                st["carry"] = carry
        return run

    def piece_rnn():
        if not has_past:
            h_last = st["carry"]
            hstate[...] = h_last

            @pl.when(t == n_t - 1)
            def _keep_h():
                ho_ref[0] = h_last

        rg = _dot(hb, win_ref[:, C_RG:C_RG + D_RNN])
        st["rnn"] = (jnp.concatenate(st["hs"], axis=0) * jax.nn.gelu(rg)).astype(BF16)

    def piece_gate_b(c):
        def run():
            cols = slice(c * GATE_COLS, (c + 1) * GATE_COLS)
            gb = _dot(hb, win_ref[:, C_GB + c * GATE_COLS:C_GB + (c + 1) * GATE_COLS])
            mrg_buf[:, cols] = jax.nn.sigmoid(gb) * _dot(st["rnn"], wbr_ref[D_ATTN:, cols])
        return run

    def piece_gate_a(c):
        def run():
            cols = slice(c * GATE_COLS, (c + 1) * GATE_COLS)
            ga = _dot(hb, win_ref[:, C_GA + c * GATE_COLS:C_GA + (c + 1) * GATE_COLS])
            sga_buf[:, cols] = jax.nn.sigmoid(ga)
        return run

    n_gate = D_MODEL // GATE_COLS
    pieces = [piece_rx, piece_gates] + [piece_scan(b) for b in range(n_chunks)] + [piece_rnn]
    pieces += [piece_gate_b(c) for c in range(n_gate)] + [piece_gate_a(c) for c in range(n_gate)]

    n_units = len(units)
    pending_s = [scores(*units[i]) for i in range(min(SCORES_AHEAD, n_units))]
    pending_e = [probs(pending_s.pop(0))]
    for i, (j, p) in enumerate(units):
        if pieces:
            pieces.pop(0)()
        weighted(j, p, *pending_e.pop(0))
        if pending_s:
            pending_e.append(probs(pending_s.pop(0)))
        if i + SCORES_AHEAD < n_units:
            pending_s.append(scores(*units[i + SCORES_AHEAD]))
    for piece in pieces:
        piece()

    if not has_past:
        for i in range(BAND_PAST // CHUNK):
            kbuf[i * CHUNK:(i + 1) * CHUNK, :] = kbuf[tm + i * CHUNK:tm + (i + 1) * CHUNK, :]
            vbuf[i * CHUNK:(i + 1) * CHUNK, :] = vbuf[tm + i * CHUNK:tm + (i + 1) * CHUNK, :]

    merged = []
    for c in range(n_gate):
        cols = slice(c * GATE_COLS, (c + 1) * GATE_COLS)
        ma = _dot(attn_buf[...], wbr_ref[0:D_ATTN, cols])
        merged.append((sga_buf[:, cols] * ma + mrg_buf[:, cols]).astype(BF16))
    merged = jnp.concatenate(merged, axis=1)
    xo_ref[...] = x + gate * _dot(merged, wout_ref[...])


def _mixer_layer(l, x, mod, mod_blk0, norm_g, w_in, bias_rows, conv_w, conv_b, gate_w, gate_b, lam,
                 w_branch, w_out, past):
    n_seq, n_tok, _ = x.shape
    has_past = past is not None
    n_seg, n_t = _tile_plan(n_seq, n_tok, has_past)
    tm = n_seg * n_tok // n_t
    n_grp = n_seq // n_seg
    keep_rows = min(BAND_PAST, n_tok)
    assert tm % CHUNK == 0 and (has_past or keep_rows % tm == 0)
    tail = CONV_W - 1
    n_gate_blk = 2 * (D_RNN // RNN_GATE_BLK)

    def layer(shape):
        return pl.BlockSpec((None,) + shape, lambda b, t: (l,) + (0,) * len(shape),
                            pipeline_mode=pl.Buffered(1))

    def per_group(shape):
        return pl.BlockSpec((n_seg,) + shape, lambda b, t: (b,) + (0,) * len(shape))

    def layer_group(shape):
        return pl.BlockSpec((None, n_seg) + shape, lambda b, t: (l, b) + (0,) * len(shape))

    in_specs = [
        pl.BlockSpec((None, tm, D_MODEL), lambda b, t: (b, t, 0)),
        pl.BlockSpec((None, n_seg, 3, D_MODEL), lambda b, t: (l, b + mod_blk0, 0, 0)),
        layer((1, D_MODEL)),
        layer((D_MODEL, D_IN)),
        layer((N_HEADS, BIAS_W)),
        layer((CONV_W, D_RNN)),
        layer((1, D_RNN)),
        layer((n_gate_blk, RNN_GATE_BLK, RNN_GATE_BLK)),
        layer((1, 2 * D_RNN)),
        layer((1, D_RNN)),
        layer((D_ATTN + D_RNN, D_MODEL)),
        layer((D_MODEL, D_MODEL)),
    ]
    args = [x.reshape(n_grp, n_seg * n_tok, D_MODEL), mod, norm_g, w_in, bias_rows, conv_w, conv_b, gate_w,
            gate_b, lam, w_branch, w_out]
    if has_past:
        in_specs += [layer_group((BAND_PAST, D_ATTN)), layer_group((BAND_PAST, D_ATTN)),
                     layer_group((tail, D_RNN)), layer_group((1, D_RNN))]
        args += list(past)
    out_specs = [
        pl.BlockSpec((None, tm, D_MODEL), lambda b, t: (b, t, 0)),
        per_group((keep_rows, D_ATTN)),
        per_group((keep_rows, D_ATTN)),
        per_group((tail, D_RNN)),
        per_group((1, D_RNN)),
    ]
    out_shape = [
        jax.ShapeDtypeStruct((n_grp, n_seg * n_tok, D_MODEL), F32),
        jax.ShapeDtypeStruct((n_seq, keep_rows, D_ATTN), F32),
        jax.ShapeDtypeStruct((n_seq, keep_rows, D_ATTN), F32),
        jax.ShapeDtypeStruct((n_seq, tail, D_RNN), F32),
        jax.ShapeDtypeStruct((n_seq, 1, D_RNN), F32),
    ]
    band_rows = n_seg * BAND if has_past else BAND_PAST + tm
    scratch = [
        pltpu.VMEM((band_rows, D_ATTN), BF16),
        pltpu.VMEM((band_rows, D_ATTN), BF16),
        pltpu.VMEM((n_seg * (SUBLANES + tm // n_seg), D_RNN), F32),
        pltpu.VMEM((1, D_RNN), F32),
        pltpu.VMEM((N_HEAD_PAIRS, PAIR_ROWS, BAND), F32),
        pltpu.VMEM((tm, D_ATTN), BF16),
        pltpu.VMEM((tm, D_MODEL), F32),
        pltpu.VMEM((tm, D_MODEL), F32),
    ]
    outs = pl.pallas_call(
        functools.partial(_mixer_kernel, tm, n_t, n_seg, has_past),
        grid=(n_grp, n_t),
        in_specs=in_specs,
        out_specs=out_specs,
        out_shape=out_shape,
        scratch_shapes=scratch,
        compiler_params=pltpu.CompilerParams(
            dimension_semantics=("parallel", "arbitrary"), vmem_limit_bytes=VMEM_LIMIT_BYTES),
        name=f"mixer_l{l}_{'sample' if has_past else 'prompt'}",
    )(*args)
    return (outs[0].reshape(n_seq, n_tok, D_MODEL),) + tuple(outs[1:])


def _ffn_kernel(tm, n_t, n_seg, has_past, final_norm, *refs):
    refs = list(refs)
    x_ref, mod_ref, g_ref, wup_ref, cw_ref, cb_ref, wdn_ref = refs[:7]
    refs = refs[7:]
    if final_norm:
        fg_ref = refs[0]
        refs = refs[1:]
    if has_past:
        cf0_ref = refs[0]
        refs = refs[1:]
    xo_ref, cfo_ref, upbuf = refs

    t = pl.program_id(1)
    seg_len = tm // n_seg
    tail = FFN_CONV_W - 1
    up_stride = SUBLANES + seg_len

    @pl.when(t == 0)
    def _init():
        if has_past:
            for i in range(n_seg):
                upbuf[i * up_stride + SUBLANES - tail:i * up_stride + SUBLANES, :] = cf0_ref[i]
        else:
            upbuf[0:SUBLANES, :] = jnp.zeros((SUBLANES, 2 * D_FF), F32)

    x = x_ref[...]
    shift, scale, gate = (_per_row(mod_ref, r, n_seg, seg_len) for r in range(3))
    hb = _modulated_norm(x, g_ref[...], shift, scale).astype(BF16)

    def project(c):
        for c0 in (c * FFN_COLS, D_FF + c * FFN_COLS):
            cols = slice(c0, c0 + FFN_COLS)
            up = _dot(hb, wup_ref[:, cols])
            for i in range(n_seg):
                base = i * up_stride + SUBLANES
                upbuf[base:base + seg_len, cols] = up[i * seg_len:(i + 1) * seg_len]

    def conv_cols(c0):
        cols = slice(c0, c0 + FFN_COLS)
        out = []
        for i in range(n_seg):
            base = i * up_stride + SUBLANES
            out.append(cb_ref[:, cols] + (cw_ref[0:1, cols] * upbuf[base - 2:base - 2 + seg_len, cols]
                                          + cw_ref[1:2, cols] * upbuf[base - 1:base - 1 + seg_len, cols]
                                          + cw_ref[2:3, cols] * upbuf[base:base + seg_len, cols]))
        return jnp.concatenate(out, axis=0)

    n_chunks = D_FF // FFN_COLS
    acc = jnp.zeros((tm, D_MODEL), F32)
    project(0)
    for c in range(n_chunks):
        if c + 1 < n_chunks:
            project(c + 1)
        val = conv_cols(c * FFN_COLS)
        gt = conv_cols(D_FF + c * FFN_COLS)
        act = (val * jax.nn.gelu(gt)).astype(BF16)
        acc = acc + _dot(act, wdn_ref[c * FFN_COLS:(c + 1) * FFN_COLS, :])

    if has_past:
        for i in range(n_seg):
            end = (i + 1) * up_stride
            cfo_ref[i] = upbuf[end - tail:end, :]
    else:
        @pl.when(t == n_t - 1)
        def _keep_conv():
            cfo_ref[0] = upbuf[SUBLANES + tm - tail:SUBLANES + tm, :]

        upbuf[0:SUBLANES, :] = upbuf[tm:tm + SUBLANES, :]

    y = x + gate * acc
    if final_norm:
        ms = jnp.mean(y * y, axis=-1, keepdims=True)
        y = y * lax.rsqrt(ms + EPS) * fg_ref[...]
    xo_ref[...] = y


def _ffn_layer(l, x, mod, mod_blk0, norm_g, w_up, conv_w, conv_b, w_down, final_g, past):
    n_seq, n_tok, _ = x.shape
    has_past = past is not None
    n_seg, n_t = _tile_plan(n_seq, n_tok, has_past)
    tm = n_seg * n_tok // n_t
    n_grp = n_seq // n_seg
    assert D_FF % FFN_COLS == 0
    final_norm = final_g is not None
    tail = FFN_CONV_W - 1

    def layer(shape):
        return pl.BlockSpec((None,) + shape, lambda b, t: (l,) + (0,) * len(shape),
                            pipeline_mode=pl.Buffered(1))

    in_specs = [
        pl.BlockSpec((None, tm, D_MODEL), lambda b, t: (b, t, 0)),
        pl.BlockSpec((None, n_seg, 3, D_MODEL), lambda b, t: (l, b + mod_blk0, 0, 0)),
        layer((1, D_MODEL)),
        layer((D_MODEL, 2 * D_FF)),
        layer((FFN_CONV_W, 2 * D_FF)),
        layer((1, 2 * D_FF)),
        layer((D_FF, D_MODEL)),
    ]
    args = [x.reshape(n_grp, n_seg * n_tok, D_MODEL), mod, norm_g, w_up, conv_w, conv_b, w_down]
    if final_norm:
        in_specs.append(pl.BlockSpec((1, D_MODEL), lambda b, t: (0, 0)))
        args.append(final_g)
    if has_past:
        in_specs.append(pl.BlockSpec((None, n_seg, tail, 2 * D_FF), lambda b, t: (l, b, 0, 0)))
        args.append(past)
    y, cf = pl.pallas_call(
        functools.partial(_ffn_kernel, tm, n_t, n_seg, has_past, final_norm),
        grid=(n_grp, n_t),
        in_specs=in_specs,
        out_specs=[
            pl.BlockSpec((None, tm, D_MODEL), lambda b, t: (b, t, 0)),
            pl.BlockSpec((n_seg, tail, 2 * D_FF), lambda b, t: (b, 0, 0)),
        ],
        out_shape=[
            jax.ShapeDtypeStruct((n_grp, n_seg * n_tok, D_MODEL), F32),
            jax.ShapeDtypeStruct((n_seq, tail, 2 * D_FF), F32),
        ],
        scratch_shapes=[pltpu.VMEM((n_seg * (SUBLANES + tm // n_seg), 2 * D_FF), F32)],
        compiler_params=pltpu.CompilerParams(
            dimension_semantics=("parallel", "arbitrary"), vmem_limit_bytes=VMEM_LIMIT_BYTES),
        name=f"ffn_l{l}_{'sample' if has_past else 'prompt'}",
    )(*args)
    return y.reshape(n_seq, n_tok, D_MODEL), cf


def _bias_rows(table):
    idx = np.clip(BAND - np.arange(BIAS_W), -MAX_REL, MAX_REL) + MAX_REL
    return jnp.transpose(table[:, idx, :], (0, 2, 1))


def _block_diag(w, width):
    depth, n, i, j = w.shape
    per = width // i
    eye = jnp.eye(per, dtype=w.dtype)
    w = w.reshape(depth, n // per, per, i, j)
    return (w[:, :, :, :, None, :] * eye[None, None, :, None, :, None]).reshape(depth, n // per, width, width)


def kernel(x_prompt, x_sample, c_prompt, c_sample, cache_k, cache_v, state_rnn_conv, state_rnn_h,
           state_ffn_conv, mod_mix_w, mod_mix_b, norm_mix_g, w_in, rel_bias_table, rnn_conv_w,
           rnn_conv_b, rnn_gate_a_w, rnn_gate_a_b, rnn_gate_x_w, rnn_gate_x_b, rnn_lambda, w_branch,
           w_out, mod_ffn_w, mod_ffn_b, norm_ffn_g, ffn_up_w, ffn_conv_w, ffn_conv_b, ffn_down_w,
           final_norm_g):
    n_prompt = x_prompt.shape[0]
    n_sample = x_sample.shape[0]

    rows = n_sample + n_prompt
    pad = (-rows) % SUBLANES
    c_all = jnp.concatenate([c_sample, c_prompt, jnp.zeros((pad, D_MODEL), F32)], axis=0)
    mod_mix = _modulation(c_all, mod_mix_w, mod_mix_b)
    mod_ffn = _modulation(c_all, mod_ffn_w, mod_ffn_b)

    w_in_b = w_in.astype(BF16)
    w_branch_b = w_branch.astype(BF16)
    w_out_b = w_out.astype(BF16)
    w_up_b = ffn_up_w.astype(BF16)
    w_down_b = ffn_down_w.astype(BF16)
    gate_w = jnp.concatenate([_block_diag(rnn_gate_a_w, RNN_GATE_BLK), _block_diag(rnn_gate_x_w, RNN_GATE_BLK)],
                             axis=1).astype(BF16)
    gate_b = jnp.concatenate([rnn_gate_a_b, rnn_gate_x_b], axis=-1).reshape(DEPTH, 1, 2 * D_RNN)
    bias_rows = _bias_rows(rel_bias_table)
    norm_mix = norm_mix_g.reshape(DEPTH, 1, D_MODEL)
    norm_ffn = norm_ffn_g.reshape(DEPTH, 1, D_MODEL)
    conv_b = rnn_conv_b.reshape(DEPTH, 1, D_RNN)
    lam = rnn_lambda.reshape(DEPTH, 1, D_RNN)
    fconv_b = ffn_conv_b.reshape(DEPTH, 1, 2 * D_FF)
    final_g = final_norm_g.reshape(1, D_MODEL)

    past_k = cache_k.reshape(DEPTH, n_sample, BAND_PAST, D_ATTN)
    past_v = cache_v.reshape(DEPTH, n_sample, BAND_PAST, D_ATTN)
    past_h = state_rnn_h.reshape(DEPTH, n_sample, 1, D_RNN)

    def run(x, mod_blk0, with_past):
        ks, vs, crs, hs, cfs = [], [], [], [], []
        for l in range(DEPTH):
            past = (past_k, past_v, state_rnn_conv, past_h) if with_past else None
            x, k_new, v_new, cr_new, h_new = _mixer_layer(
                l, x, mod_mix, mod_blk0, norm_mix, w_in_b, bias_rows, rnn_conv_w, conv_b, gate_w, gate_b,
                lam, w_branch_b, w_out_b, past)
            x, cf_new = _ffn_layer(
                l, x, mod_ffn, mod_blk0, norm_ffn, w_up_b, ffn_conv_w, fconv_b, w_down_b,
                final_g if l == DEPTH - 1 else None, state_ffn_conv if with_past else None)
            ks.append(k_new.reshape(k_new.shape[0], k_new.shape[1], N_HEADS, HEAD_DIM))
            vs.append(v_new.reshape(v_new.shape[0], v_new.shape[1], N_HEADS, HEAD_DIM))
            crs.append(cr_new)
            hs.append(h_new[:, 0, :])
            cfs.append(cf_new)
        return x, jnp.stack(ks), jnp.stack(vs), jnp.stack(crs), jnp.stack(hs), jnp.stack(cfs)

    y_s, k_s, v_s, rc_s, h_s, fc_s = run(x_sample, 0, True)
    y_p, k_p, v_p, rc_p, h_p, fc_p = run(x_prompt, n_sample, False)
    return (y_p, y_s, k_p, v_p, k_s, v_s, rc_p, rc_s, h_p, h_s, fc_p, fc_s)
```

```python
import functools

import jax
import jax.numpy as jnp
import numpy as np
from jax import lax
from jax.experimental import pallas as pl
from jax.experimental.pallas import tpu as pltpu

F32 = jnp.float32
BF16 = jnp.bfloat16

D_MODEL = 1024
DEPTH = 2
CHUNK = 64
N_BAND_PAST = 8
BAND_PAST = N_BAND_PAST * CHUNK
BAND = BAND_PAST + CHUNK
N_HEADS = 8
HEAD_DIM = 64
D_ATTN = N_HEADS * HEAD_DIM
MAX_REL = 256
D_RNN = 512
N_RNN_BLOCKS = 8
CONV_W = 4
RG_C = 8.0
D_FF = 2816
FFN_CONV_W = 3
EPS = 1e-6
NEG_INF = -1e30

SUBLANES = 8
LANES = 128
MXU_TILE = 256
HEADS_PER_VREG = LANES // HEAD_DIM
N_HEAD_PAIRS = N_HEADS // HEADS_PER_VREG
PAIR_ROWS = HEADS_PER_VREG * CHUNK
BIAS_W = BAND + CHUNK
FFN_COLS = MXU_TILE
GATE_COLS = MXU_TILE
RNN_GATE_BLK = MXU_TILE
SCORES_AHEAD = 3
UP_AHEAD = 3
TILE_ROWS = 256
VMEM_LIMIT_BYTES = 56 * 1024 * 1024

C_Q, C_K, C_V = 0, D_ATTN, 2 * D_ATTN
C_RX = 3 * D_ATTN
C_RG = C_RX + D_RNN
C_GA = C_RG + D_RNN
C_GB = C_GA + D_MODEL
D_IN = C_GB + D_MODEL


def _dot(a, b):
    return jnp.dot(a, b, preferred_element_type=F32)


def _modulated_norm(x, g, shift, scale):
    ms = jnp.mean(x * x, axis=-1, keepdims=True)
    y = x * lax.rsqrt(ms + EPS)
    return (y * g) * (1.0 + scale) + shift


def _per_row(mod_ref, r, n_seg, seg_len):
    if n_seg == 1:
        return mod_ref[0, r:r + 1, :]
    return jnp.concatenate(
        [jnp.broadcast_to(mod_ref[i, r:r + 1, :], (seg_len, D_MODEL)) for i in range(n_seg)], axis=0)


def _tile_plan(n_seq, n_tok, independent):
    if independent:
        n_seg = max(1, min(n_seq, TILE_ROWS // n_tok))
        assert n_seq % n_seg == 0 and n_tok == CHUNK
        return n_seg, 1
    assert n_tok % TILE_ROWS == 0
    return 1, n_tok // TILE_ROWS


def _mod_kernel(c_ref, w_ref, b_ref, o_ref):
    c = c_ref[...]
    s = (c * jax.nn.sigmoid(c)).astype(BF16)
    o_ref[...] = _dot(s, w_ref[...].astype(BF16)) + b_ref[...]


def _modulation(c_all, w, b):
    rows = c_all.shape[0]
    n_col = (3 * D_MODEL) // D_MODEL
    out = pl.pallas_call(
        _mod_kernel,
        grid=(DEPTH, n_col),
        in_specs=[
            pl.BlockSpec((rows, D_MODEL), lambda l, j: (0, 0)),
            pl.BlockSpec((None, D_MODEL, D_MODEL), lambda l, j: (l, 0, j)),
            pl.BlockSpec((None, 1, D_MODEL), lambda l, j: (l, 0, j)),
        ],
        out_specs=pl.BlockSpec((None, rows, D_MODEL), lambda l, j: (l, 0, j)),
        out_shape=jax.ShapeDtypeStruct((DEPTH, rows, 3 * D_MODEL), F32),
        compiler_params=pltpu.CompilerParams(dimension_semantics=("parallel", "parallel")),
        name="adaln_mod",
    )(c_all, w, b.reshape(DEPTH, 1, 3 * D_MODEL))
    return out.reshape(DEPTH, rows, 3, D_MODEL)


def _expand_bias(g_ref, bias_buf):
    for p in range(N_HEAD_PAIRS):
        rows = []
        for hh in range(HEADS_PER_VREG):
            h = HEADS_PER_VREG * p + hh
            g = jnp.broadcast_to(g_ref[h:h + 1, :], (CHUNK, BIAS_W))
            rows.append(pltpu.roll(g, BAND, axis=1, stride=1, stride_axis=0)[:, :BAND])
        bias_buf[p] = jnp.concatenate(rows, axis=0)


def _scan_groups(a, u):
    nv = a.shape[0] // SUBLANES
    a3 = a.reshape(nv, SUBLANES, D_RNN)
    u3 = u.reshape(nv, SUBLANES, D_RNN)
    sub = lax.broadcasted_iota(jnp.int32, (nv, SUBLANES, D_RNN), 1)
    d = 1
    while d < SUBLANES:
        keep = sub >= d
        a_s = pltpu.roll(a3, d, axis=1)
        u_s = pltpu.roll(u3, d, axis=1)
        u3 = jnp.where(keep, a3 * u_s + u3, u3)
        a3 = jnp.where(keep, a3 * a_s, a3)
        d *= 2
    return a3, u3


def _mixer_kernel(tm, n_t, n_seg, has_past, *refs):
    refs = list(refs)
    (x_ref, mod_ref, g_ref, win_ref, brow_ref, cw_ref, cb_ref, gw_ref, gb_ref, lam_ref,
     wbr_ref, wout_ref) = refs[:12]
    refs = refs[12:]
    if has_past:
        pk_ref, pv_ref, cr0_ref, h0_ref = refs[:4]
        refs = refs[4:]
    (xo_ref, ko_ref, vo_ref, cro_ref, ho_ref,
     kbuf, vbuf, rxbuf, hstate, bias_buf, attn_buf, mrg_buf, sga_buf) = refs

    t = pl.program_id(1)
    seg_len = tm // n_seg
    n_chunks = tm // CHUNK
    tail = CONV_W - 1
    rx_stride = SUBLANES + seg_len
    band_stride = BAND if has_past else CHUNK

    @pl.when(t == 0)
    def _init():
        _expand_bias(brow_ref, bias_buf)
        if has_past:
            for i in range(n_seg):
                kbuf[i * BAND:i * BAND + BAND_PAST, :] = pk_ref[i].astype(BF16)
                vbuf[i * BAND:i * BAND + BAND_PAST, :] = pv_ref[i].astype(BF16)
                rxbuf[i * rx_stride + SUBLANES - tail:i * rx_stride + SUBLANES, :] = cr0_ref[i]
        else:
            kbuf[0:BAND_PAST, :] = jnp.zeros((BAND_PAST, D_ATTN), BF16)
            vbuf[0:BAND_PAST, :] = jnp.zeros((BAND_PAST, D_ATTN), BF16)
            rxbuf[0:SUBLANES, :] = jnp.zeros((SUBLANES, D_RNN), F32)
            hstate[...] = jnp.zeros((1, D_RNN), F32)

    x = x_ref[...]
    shift, scale, gate = (_per_row(mod_ref, r, n_seg, seg_len) for r in range(3))
    hb = _modulated_norm(x, g_ref[...], shift, scale).astype(BF16)

    k = _dot(hb, win_ref[:, C_K:C_K + D_ATTN])
    v = _dot(hb, win_ref[:, C_V:C_V + D_ATTN])
    if has_past:
        for i in range(n_seg):
            rows = slice(i * CHUNK, (i + 1) * CHUNK)
            kbuf[i * BAND + BAND_PAST:(i + 1) * BAND, :] = k[rows].astype(BF16)
            vbuf[i * BAND + BAND_PAST:(i + 1) * BAND, :] = v[rows].astype(BF16)
            ko_ref[i] = k[rows]
            vo_ref[i] = v[rows]
    else:
        kbuf[BAND_PAST:BAND_PAST + tm, :] = k.astype(BF16)
        vbuf[BAND_PAST:BAND_PAST + tm, :] = v.astype(BF16)
        first_keep = n_t - BAND_PAST // tm

        @pl.when(t >= first_keep)
        def _keep_kv():
            off = pl.multiple_of((t - first_keep) * tm, tm)
            ko_ref[0, pl.ds(off, tm), :] = k
            vo_ref[0, pl.ds(off, tm), :] = v

    q = _dot(hb, win_ref[:, C_Q:C_Q + D_ATTN]) * (HEAD_DIM ** -0.5)

    lane = lax.broadcasted_iota(jnp.int32, (CHUNK, LANES), 1)
    low = lane < HEAD_DIM
    units = [(j, p) for j in range(n_chunks) for p in range(N_HEAD_PAIRS)]

    def scores(j, p):
        cols = slice(LANES * p, LANES * (p + 1))
        qp = q[j * CHUNK:(j + 1) * CHUNK, cols]
        qs = jnp.concatenate(
            [jnp.where(low, qp, 0.0).astype(BF16), jnp.where(low, 0.0, qp).astype(BF16)], axis=0)
        kp = kbuf[j * band_stride:j * band_stride + BAND, cols]
        s = lax.dot_general(qs, kp, (((1,), (1,)), ((), ())), preferred_element_type=F32)
        s = s + bias_buf[p]
        if not has_past:
            col = lax.broadcasted_iota(jnp.int32, (PAIR_ROWS, BAND), 1)
            s = jnp.where(col >= BAND_PAST - CHUNK * (t * n_chunks + j), s, NEG_INF)
        return s

    def probs(s):
        m = jnp.max(s, axis=-1, keepdims=True)
        e = jnp.exp(s - m)
        return e.astype(BF16), jnp.sum(e, axis=-1, keepdims=True)

    def weighted(j, p, e, denom):
        cols = slice(LANES * p, LANES * (p + 1))
        o = _dot(e, vbuf[j * band_stride:j * band_stride + BAND, cols]) / denom
        attn_buf[j * CHUNK:(j + 1) * CHUNK, cols] = jnp.where(low, o[:CHUNK], o[CHUNK:]).astype(BF16)

    st = {}

    def piece_rx():
        rx = _dot(hb, win_ref[:, C_RX:C_RX + D_RNN])
        xc = []
        for i in range(n_seg):
            base = i * rx_stride + SUBLANES
            rxs = rx[i * seg_len:(i + 1) * seg_len]
            rxbuf[base:base + seg_len, :] = rxs
            xc.append(cb_ref[...] + (cw_ref[0:1, :] * rxbuf[base - 3:base - 3 + seg_len, :]
                                     + cw_ref[1:2, :] * rxbuf[base - 2:base - 2 + seg_len, :]
                                     + cw_ref[2:3, :] * rxbuf[base - 1:base - 1 + seg_len, :]
                                     + cw_ref[3:4, :] * rxs))
            if has_past:
                cro_ref[i] = rxbuf[base + seg_len - tail:base + seg_len, :]
        st["xc"] = jnp.concatenate(xc, axis=0)
        if not has_past:
            @pl.when(t == n_t - 1)
            def _keep_conv():
                cro_ref[0] = rxbuf[SUBLANES + tm - tail:SUBLANES + tm, :]

            rxbuf[0:SUBLANES, :] = rxbuf[tm:tm + SUBLANES, :]

    def piece_gates():
        xcb = st["xc"].astype(BF16)
        n_blk = D_RNN // RNN_GATE_BLK
        gates = [_dot(xcb[:, (i % n_blk) * RNN_GATE_BLK:(i % n_blk + 1) * RNN_GATE_BLK], gw_ref[i])
                 for i in range(2 * n_blk)]
        st["gates"] = jnp.concatenate(gates, axis=1) + gb_ref[...]
        st["carry"] = None if has_past else hstate[...]
        st["hs"] = []

    def piece_scan(blk):
        def run():
            rows = slice(blk * CHUNK, (blk + 1) * CHUNK)
            gates = st["gates"][rows, :]
            xc = st["xc"][rows, :]
            r = jax.nn.sigmoid(gates[:, :D_RNN])
            ig = jax.nn.sigmoid(gates[:, D_RNN:])
            log_a = -RG_C * r * jax.nn.softplus(-lam_ref[...])
            a = jnp.exp(log_a)
            u = jnp.sqrt(-jnp.tanh(log_a) * (a * a + 1.0)) * (ig * xc)
            a3, u3 = _scan_groups(a, u)
            carry = h0_ref[blk] if has_past else st["carry"]
            for i in range(CHUNK // SUBLANES):
                hi = u3[i] + a3[i] * carry
                st["hs"].append(hi)
                carry = hi[SUBLANES - 1:SUBLANES, :]
            if has_past:
                ho_ref[blk] = carry
            else:
                st["carry"] = carry
        return run

    def piece_rnn():
        if not has_past:
            h_last = st["carry"]
            hstate[...] = h_last

            @pl.when(t == n_t - 1)
            def _keep_h():
                ho_ref[0] = h_last

        rg = _dot(hb, win_ref[:, C_RG:C_RG + D_RNN])
        st["rnn"] = (jnp.concatenate(st["hs"], axis=0) * jax.nn.gelu(rg)).astype(BF16)

    def piece_gate_b(c):
        def run():
            cols = slice(c * GATE_COLS, (c + 1) * GATE_COLS)
            gb = _dot(hb, win_ref[:, C_GB + c * GATE_COLS:C_GB + (c + 1) * GATE_COLS])
            mrg_buf[:, cols] = jax.nn.sigmoid(gb) * _dot(st["rnn"], wbr_ref[D_ATTN:, cols])
        return run

    def piece_gate_a(c):
        def run():
            cols = slice(c * GATE_COLS, (c + 1) * GATE_COLS)
            ga = _dot(hb, win_ref[:, C_GA + c * GATE_COLS:C_GA + (c + 1) * GATE_COLS])
            sga_buf[:, cols] = jax.nn.sigmoid(ga)
        return run

    n_gate = D_MODEL // GATE_COLS
    pieces = [piece_rx, piece_gates] + [piece_scan(b) for b in range(n_chunks)] + [piece_rnn]
    pieces += [piece_gate_b(c) for c in range(n_gate)] + [piece_gate_a(c) for c in range(n_gate)]

    n_units = len(units)
    pending_s = [scores(*units[i]) for i in range(min(SCORES_AHEAD, n_units))]
    pending_e = [probs(pending_s.pop(0))]
    for i, (j, p) in enumerate(units):
        if pieces:
            pieces.pop(0)()
        weighted(j, p, *pending_e.pop(0))
        if pending_s:
            pending_e.append(probs(pending_s.pop(0)))
        if i + SCORES_AHEAD < n_units:
            pending_s.append(scores(*units[i + SCORES_AHEAD]))
    for piece in pieces:
        piece()

    if not has_past:
        for i in range(BAND_PAST // CHUNK):
            kbuf[i * CHUNK:(i + 1) * CHUNK, :] = kbuf[tm + i * CHUNK:tm + (i + 1) * CHUNK, :]
            vbuf[i * CHUNK:(i + 1) * CHUNK, :] = vbuf[tm + i * CHUNK:tm + (i + 1) * CHUNK, :]

    merged = []
    for c in range(n_gate):
        cols = slice(c * GATE_COLS, (c + 1) * GATE_COLS)
        ma = _dot(attn_buf[...], wbr_ref[0:D_ATTN, cols])
        merged.append((sga_buf[:, cols] * ma + mrg_buf[:, cols]).astype(BF16))
    merged = jnp.concatenate(merged, axis=1)
    xo_ref[...] = x + gate * _dot(merged, wout_ref[...])


def _mixer_layer(l, x, mod, mod_blk0, norm_g, w_in, bias_rows, conv_w, conv_b, gate_w, gate_b, lam,
                 w_branch, w_out, past):
    n_seq, n_tok, _ = x.shape
    has_past = past is not None
    n_seg, n_t = _tile_plan(n_seq, n_tok, has_past)
    tm = n_seg * n_tok // n_t
    n_grp = n_seq // n_seg
    keep_rows = min(BAND_PAST, n_tok)
    assert tm % CHUNK == 0 and (has_past or keep_rows % tm == 0)
    tail = CONV_W - 1
    n_gate_blk = 2 * (D_RNN // RNN_GATE_BLK)

    def layer(shape):
        return pl.BlockSpec((None,) + shape, lambda b, t: (l,) + (0,) * len(shape),
                            pipeline_mode=pl.Buffered(1))

    def per_group(shape):
        return pl.BlockSpec((n_seg,) + shape, lambda b, t: (b,) + (0,) * len(shape))

    def layer_group(shape):
        return pl.BlockSpec((None, n_seg) + shape, lambda b, t: (l, b) + (0,) * len(shape))

    in_specs = [
        pl.BlockSpec((None, tm, D_MODEL), lambda b, t: (b, t, 0)),
        pl.BlockSpec((None, n_seg, 3, D_MODEL), lambda b, t: (l, b + mod_blk0, 0, 0)),
        layer((1, D_MODEL)),
        layer((D_MODEL, D_IN)),
        layer((N_HEADS, BIAS_W)),
        layer((CONV_W, D_RNN)),
        layer((1, D_RNN)),
        layer((n_gate_blk, RNN_GATE_BLK, RNN_GATE_BLK)),
        layer((1, 2 * D_RNN)),
        layer((1, D_RNN)),
        layer((D_ATTN + D_RNN, D_MODEL)),
        layer((D_MODEL, D_MODEL)),
    ]
    args = [x.reshape(n_grp, n_seg * n_tok, D_MODEL), mod, norm_g, w_in, bias_rows, conv_w, conv_b, gate_w,
            gate_b, lam, w_branch, w_out]
    if has_past:
        in_specs += [layer_group((BAND_PAST, D_ATTN)), layer_group((BAND_PAST, D_ATTN)),
                     layer_group((tail, D_RNN)), layer_group((1, D_RNN))]
        args += list(past)
    out_specs = [
        pl.BlockSpec((None, tm, D_MODEL), lambda b, t: (b, t, 0)),
        per_group((keep_rows, D_ATTN)),
        per_group((keep_rows, D_ATTN)),
        per_group((tail, D_RNN)),
        per_group((1, D_RNN)),
    ]
    out_shape = [
        jax.ShapeDtypeStruct((n_grp, n_seg * n_tok, D_MODEL), F32),
        jax.ShapeDtypeStruct((n_seq, keep_rows, D_ATTN), F32),
        jax.ShapeDtypeStruct((n_seq, keep_rows, D_ATTN), F32),
        jax.ShapeDtypeStruct((n_seq, tail, D_RNN), F32),
        jax.ShapeDtypeStruct((n_seq, 1, D_RNN), F32),
    ]
    band_rows = n_seg * BAND if has_past else BAND_PAST + tm
    scratch = [
        pltpu.VMEM((band_rows, D_ATTN), BF16),
        pltpu.VMEM((band_rows, D_ATTN), BF16),
        pltpu.VMEM((n_seg * (SUBLANES + tm // n_seg), D_RNN), F32),
        pltpu.VMEM((1, D_RNN), F32),
        pltpu.VMEM((N_HEAD_PAIRS, PAIR_ROWS, BAND), F32),
        pltpu.VMEM((tm, D_ATTN), BF16),
        pltpu.VMEM((tm, D_MODEL), F32),
        pltpu.VMEM((tm, D_MODEL), F32),
    ]
    outs = pl.pallas_call(
        functools.partial(_mixer_kernel, tm, n_t, n_seg, has_past),
        grid=(n_grp, n_t),
        in_specs=in_specs,
        out_specs=out_specs,
        out_shape=out_shape,
        scratch_shapes=scratch,
        compiler_params=pltpu.CompilerParams(
            dimension_semantics=("parallel", "arbitrary"), vmem_limit_bytes=VMEM_LIMIT_BYTES),
        name=f"mixer_l{l}_{'sample' if has_past else 'prompt'}",
    )(*args)
    return (outs[0].reshape(n_seq, n_tok, D_MODEL),) + tuple(outs[1:])


def _ffn_kernel(tm, n_t, n_seg, has_past, final_norm, *refs):
    refs = list(refs)
    x_ref, mod_ref, g_ref, wup_ref, cw_ref, cb_ref, wdn_ref = refs[:7]
    refs = refs[7:]
    if final_norm:
        fg_ref = refs[0]
        refs = refs[1:]
    if has_past:
        cf0_ref = refs[0]
        refs = refs[1:]
    xo_ref, cfo_ref, hist = refs

    t = pl.program_id(1)
    seg_len = tm // n_seg
    tail = FFN_CONV_W - 1

    @pl.when(t == 0)
    def _init():
        hist[...] = jnp.zeros((n_seg * SUBLANES, 2 * D_FF), F32)
        if has_past:
            for i in range(n_seg):
                hist[(i + 1) * SUBLANES - tail:(i + 1) * SUBLANES, :] = cf0_ref[i]

    x = x_ref[...]
    shift, scale, gate = (_per_row(mod_ref, r, n_seg, seg_len) for r in range(3))
    hb = _modulated_norm(x, g_ref[...], shift, scale).astype(BF16)

    def project(c):
        return [_dot(hb, wup_ref[:, c0:c0 + FFN_COLS]) for c0 in (c * FFN_COLS, D_FF + c * FFN_COLS)]

    def conv_cols(c0, up):
        cols = slice(c0, c0 + FFN_COLS)
        out = []
        for i in range(n_seg):
            cur = up[i * seg_len:(i + 1) * seg_len]
            ext = jnp.concatenate([hist[i * SUBLANES:(i + 1) * SUBLANES, cols], cur], axis=0)
            back1 = pltpu.roll(ext, 1, axis=0)[SUBLANES:]
            back2 = pltpu.roll(ext, 2, axis=0)[SUBLANES:]
            out.append(cb_ref[:, cols] + (cw_ref[0:1, cols] * back2 + cw_ref[1:2, cols] * back1
                                          + cw_ref[2:3, cols] * cur))
            hist[i * SUBLANES:(i + 1) * SUBLANES, cols] = cur[seg_len - SUBLANES:]
        return jnp.concatenate(out, axis=0)

    n_chunks = D_FF // FFN_COLS
    acc = jnp.zeros((tm, D_MODEL), F32)
    ahead = [project(c) for c in range(min(UP_AHEAD, n_chunks))]
    for c in range(n_chunks):
        up_val, up_gate = ahead.pop(0)
        if c + UP_AHEAD < n_chunks:
            ahead.append(project(c + UP_AHEAD))
        val = conv_cols(c * FFN_COLS, up_val)
        gt = conv_cols(D_FF + c * FFN_COLS, up_gate)
        act = (val * jax.nn.gelu(gt)).astype(BF16)
        acc = acc + _dot(act, wdn_ref[c * FFN_COLS:(c + 1) * FFN_COLS, :])

    if has_past:
        for i in range(n_seg):
            cfo_ref[i] = hist[(i + 1) * SUBLANES - tail:(i + 1) * SUBLANES, :]
    else:
        @pl.when(t == n_t - 1)
        def _keep_conv():
            cfo_ref[0] = hist[SUBLANES - tail:SUBLANES, :]

    y = x + gate * acc
    if final_norm:
        ms = jnp.mean(y * y, axis=-1, keepdims=True)
        y = y * lax.rsqrt(ms + EPS) * fg_ref[...]
    xo_ref[...] = y


def _ffn_layer(l, x, mod, mod_blk0, norm_g, w_up, conv_w, conv_b, w_down, final_g, past):
    n_seq, n_tok, _ = x.shape
    has_past = past is not None
    n_seg, n_t = _tile_plan(n_seq, n_tok, has_past)
    tm = n_seg * n_tok // n_t
    n_grp = n_seq // n_seg
    assert D_FF % FFN_COLS == 0
    final_norm = final_g is not None
    tail = FFN_CONV_W - 1

    def layer(shape):
        return pl.BlockSpec((None,) + shape, lambda b, t: (l,) + (0,) * len(shape),
                            pipeline_mode=pl.Buffered(1))

    in_specs = [
        pl.BlockSpec((None, tm, D_MODEL), lambda b, t: (b, t, 0)),
        pl.BlockSpec((None, n_seg, 3, D_MODEL), lambda b, t: (l, b + mod_blk0, 0, 0)),
        layer((1, D_MODEL)),
        layer((D_MODEL, 2 * D_FF)),
        layer((FFN_CONV_W, 2 * D_FF)),
        layer((1, 2 * D_FF)),
        layer((D_FF, D_MODEL)),
    ]
    args = [x.reshape(n_grp, n_seg * n_tok, D_MODEL), mod, norm_g, w_up, conv_w, conv_b, w_down]
    if final_norm:
        in_specs.append(pl.BlockSpec((1, D_MODEL), lambda b, t: (0, 0)))
        args.append(final_g)
    if has_past:
        in_specs.append(pl.BlockSpec((None, n_seg, tail, 2 * D_FF), lambda b, t: (l, b, 0, 0)))
        args.append(past)
    y, cf = pl.pallas_call(
        functools.partial(_ffn_kernel, tm, n_t, n_seg, has_past, final_norm),
        grid=(n_grp, n_t),
        in_specs=in_specs,
        out_specs=[
            pl.BlockSpec((None, tm, D_MODEL), lambda b, t: (b, t, 0)),
            pl.BlockSpec((n_seg, tail, 2 * D_FF), lambda b, t: (b, 0, 0)),
        ],
        out_shape=[
            jax.ShapeDtypeStruct((n_grp, n_seg * n_tok, D_MODEL), F32),
            jax.ShapeDtypeStruct((n_seq, tail, 2 * D_FF), F32),
        ],
        scratch_shapes=[pltpu.VMEM((n_seg * SUBLANES, 2 * D_FF), F32)],
        compiler_params=pltpu.CompilerParams(
            dimension_semantics=("parallel", "arbitrary"), vmem_limit_bytes=VMEM_LIMIT_BYTES),
        name=f"ffn_l{l}_{'sample' if has_past else 'prompt'}",
    )(*args)
    return y.reshape(n_seq, n_tok, D_MODEL), cf


def _bias_rows(table):
    idx = np.clip(BAND - np.arange(BIAS_W), -MAX_REL, MAX_REL) + MAX_REL
    return jnp.transpose(table[:, idx, :], (0, 2, 1))


def _block_diag(w, width):
    depth, n, i, j = w.shape
    per = width // i
    eye = jnp.eye(per, dtype=w.dtype)
    w = w.reshape(depth, n // per, per, i, j)
    return (w[:, :, :, :, None, :] * eye[None, None, :, None, :, None]).reshape(depth, n // per, width, width)


def kernel(x_prompt, x_sample, c_prompt, c_sample, cache_k, cache_v, state_rnn_conv, state_rnn_h,
           state_ffn_conv, mod_mix_w, mod_mix_b, norm_mix_g, w_in, rel_bias_table, rnn_conv_w,
           rnn_conv_b, rnn_gate_a_w, rnn_gate_a_b, rnn_gate_x_w, rnn_gate_x_b, rnn_lambda, w_branch,
           w_out, mod_ffn_w, mod_ffn_b, norm_ffn_g, ffn_up_w, ffn_conv_w, ffn_conv_b, ffn_down_w,
           final_norm_g):
    n_prompt = x_prompt.shape[0]
    n_sample = x_sample.shape[0]

    rows = n_sample + n_prompt
    pad = (-rows) % SUBLANES
    c_all = jnp.concatenate([c_sample, c_prompt, jnp.zeros((pad, D_MODEL), F32)], axis=0)
    mod_mix = _modulation(c_all, mod_mix_w, mod_mix_b)
    mod_ffn = _modulation(c_all, mod_ffn_w, mod_ffn_b)

    w_in_b = w_in.astype(BF16)
    w_branch_b = w_branch.astype(BF16)
    w_out_b = w_out.astype(BF16)
    w_up_b = ffn_up_w.astype(BF16)
    w_down_b = ffn_down_w.astype(BF16)
    gate_w = jnp.concatenate([_block_diag(rnn_gate_a_w, RNN_GATE_BLK), _block_diag(rnn_gate_x_w, RNN_GATE_BLK)],
                             axis=1).astype(BF16)
    gate_b = jnp.concatenate([rnn_gate_a_b, rnn_gate_x_b], axis=-1).reshape(DEPTH, 1, 2 * D_RNN)
    bias_rows = _bias_rows(rel_bias_table)
    norm_mix = norm_mix_g.reshape(DEPTH, 1, D_MODEL)
    norm_ffn = norm_ffn_g.reshape(DEPTH, 1, D_MODEL)
    conv_b = rnn_conv_b.reshape(DEPTH, 1, D_RNN)
    lam = rnn_lambda.reshape(DEPTH, 1, D_RNN)
    fconv_b = ffn_conv_b.reshape(DEPTH, 1, 2 * D_FF)
    final_g = final_norm_g.reshape(1, D_MODEL)

    past_k = cache_k.reshape(DEPTH, n_sample, BAND_PAST, D_ATTN)
    past_v = cache_v.reshape(DEPTH, n_sample, BAND_PAST, D_ATTN)
    past_h = state_rnn_h.reshape(DEPTH, n_sample, 1, D_RNN)

    def run(x, mod_blk0, with_past):
        ks, vs, crs, hs, cfs = [], [], [], [], []
        for l in range(DEPTH):
            past = (past_k, past_v, state_rnn_conv, past_h) if with_past else None
            x, k_new, v_new, cr_new, h_new = _mixer_layer(
                l, x, mod_mix, mod_blk0, norm_mix, w_in_b, bias_rows, rnn_conv_w, conv_b, gate_w, gate_b,
                lam, w_branch_b, w_out_b, past)
            x, cf_new = _ffn_layer(
                l, x, mod_ffn, mod_blk0, norm_ffn, w_up_b, ffn_conv_w, fconv_b, w_down_b,
                final_g if l == DEPTH - 1 else None, state_ffn_conv if with_past else None)
            ks.append(k_new.reshape(k_new.shape[0], k_new.shape[1], N_HEADS, HEAD_DIM))
            vs.append(v_new.reshape(v_new.shape[0], v_new.shape[1], N_HEADS, HEAD_DIM))
            crs.append(cr_new)
            hs.append(h_new[:, 0, :])
            cfs.append(cf_new)
        return x, jnp.stack(ks), jnp.stack(vs), jnp.stack(crs), jnp.stack(hs), jnp.stack(cfs)

    y_s, k_s, v_s, rc_s, h_s, fc_s = run(x_sample, 0, True)
    y_p, k_p, v_p, rc_p, h_p, fc_p = run(x_prompt, n_sample, False)
    return (y_p, y_s, k_p, v_p, k_s, v_s, rc_p, rc_s, h_p, h_s, fc_p, fc_s)
```

```python
import functools

import jax
import jax.numpy as jnp
import numpy as np
from jax import lax
from jax.experimental import pallas as pl
from jax.experimental.pallas import tpu as pltpu

F32 = jnp.float32
BF16 = jnp.bfloat16

D_MODEL = 1024
DEPTH = 2
CHUNK = 64
N_BAND_PAST = 8
BAND_PAST = N_BAND_PAST * CHUNK
BAND = BAND_PAST + CHUNK
N_HEADS = 8
HEAD_DIM = 64
D_ATTN = N_HEADS * HEAD_DIM
MAX_REL = 256
D_RNN = 512
N_RNN_BLOCKS = 8
CONV_W = 4
RG_C = 8.0
D_FF = 2816
FFN_CONV_W = 3
EPS = 1e-6
NEG_INF = -1e30

SUBLANES = 8
LANES = 128
MXU_TILE = 256
HEADS_PER_VREG = LANES // HEAD_DIM
N_HEAD_PAIRS = N_HEADS // HEADS_PER_VREG
PAIR_ROWS = HEADS_PER_VREG * CHUNK
BIAS_W = BAND + CHUNK
FFN_COLS = MXU_TILE
GATE_COLS = MXU_TILE
RNN_GATE_BLK = MXU_TILE
SCORES_AHEAD = 3
UP_AHEAD = 3
TILE_ROWS = 256
MIXER_LONG_ROWS = 512
FFN_LONG_ROWS = 256
VMEM_LIMIT_BYTES = 56 * 1024 * 1024

C_Q, C_K, C_V = 0, D_ATTN, 2 * D_ATTN
C_RX = 3 * D_ATTN
C_RG = C_RX + D_RNN
C_GA = C_RG + D_RNN
C_GB = C_GA + D_MODEL
D_IN = C_GB + D_MODEL


def _dot(a, b):
    return jnp.dot(a, b, preferred_element_type=F32)


def _modulated_norm(x, g, shift, scale):
    ms = jnp.mean(x * x, axis=-1, keepdims=True)
    y = x * lax.rsqrt(ms + EPS)
    return (y * g) * (1.0 + scale) + shift


def _per_row(mod_ref, r, n_seg, seg_len):
    if n_seg == 1:
        return mod_ref[0, r:r + 1, :]
    return jnp.concatenate(
        [jnp.broadcast_to(mod_ref[i, r:r + 1, :], (seg_len, D_MODEL)) for i in range(n_seg)], axis=0)


def _tile_plan(n_seq, n_tok, independent, long_rows):
    if independent:
        n_seg = max(1, min(n_seq, TILE_ROWS // n_tok))
        assert n_seq % n_seg == 0 and n_tok == CHUNK
        return n_seg, 1
    assert n_tok % long_rows == 0
    return 1, n_tok // long_rows


def _mod_kernel(c_ref, w_ref, b_ref, o_ref):
    c = c_ref[...]
    s = (c * jax.nn.sigmoid(c)).astype(BF16)
    o_ref[...] = _dot(s, w_ref[...].astype(BF16)) + b_ref[...]


def _modulation(c_all, w, b):
    rows = c_all.shape[0]
    n_col = (3 * D_MODEL) // D_MODEL
    out = pl.pallas_call(
        _mod_kernel,
        grid=(DEPTH, n_col),
        in_specs=[
            pl.BlockSpec((rows, D_MODEL), lambda l, j: (0, 0)),
            pl.BlockSpec((None, D_MODEL, D_MODEL), lambda l, j: (l, 0, j)),
            pl.BlockSpec((None, 1, D_MODEL), lambda l, j: (l, 0, j)),
        ],
        out_specs=pl.BlockSpec((None, rows, D_MODEL), lambda l, j: (l, 0, j)),
        out_shape=jax.ShapeDtypeStruct((DEPTH, rows, 3 * D_MODEL), F32),
        compiler_params=pltpu.CompilerParams(dimension_semantics=("parallel", "parallel")),
        name="adaln_mod",
    )(c_all, w, b.reshape(DEPTH, 1, 3 * D_MODEL))
    return out.reshape(DEPTH, rows, 3, D_MODEL)


def _expand_bias(g_ref, bias_buf):
    for p in range(N_HEAD_PAIRS):
        rows = []
        for hh in range(HEADS_PER_VREG):
            h = HEADS_PER_VREG * p + hh
            g = jnp.broadcast_to(g_ref[h:h + 1, :], (CHUNK, BIAS_W))
            rows.append(pltpu.roll(g, BAND, axis=1, stride=1, stride_axis=0)[:, :BAND])
        bias_buf[p] = jnp.concatenate(rows, axis=0)


def _scan_groups(a, u):
    nv = a.shape[0] // SUBLANES
    a3 = a.reshape(nv, SUBLANES, D_RNN)
    u3 = u.reshape(nv, SUBLANES, D_RNN)
    sub = lax.broadcasted_iota(jnp.int32, (nv, SUBLANES, D_RNN), 1)
    d = 1
    while d < SUBLANES:
        keep = sub >= d
        a_s = pltpu.roll(a3, d, axis=1)
        u_s = pltpu.roll(u3, d, axis=1)
        u3 = jnp.where(keep, a3 * u_s + u3, u3)
        a3 = jnp.where(keep, a3 * a_s, a3)
        d *= 2
    return a3, u3


def _mixer_kernel(tm, n_t, n_seg, has_past, *refs):
    refs = list(refs)
    (x_ref, mod_ref, g_ref, win_ref, brow_ref, cw_ref, cb_ref, gw_ref, gb_ref, lam_ref,
     wbr_ref, wout_ref) = refs[:12]
    refs = refs[12:]
    if has_past:
        pk_ref, pv_ref, cr0_ref, h0_ref = refs[:4]
        refs = refs[4:]
    (xo_ref, ko_ref, vo_ref, cro_ref, ho_ref,
     kbuf, vbuf, rxbuf, hstate, bias_buf, attn_buf, mrg_buf, sga_buf) = refs

    t = pl.program_id(1)
    seg_len = tm // n_seg
    n_chunks = tm // CHUNK
    tail = CONV_W - 1
    rx_stride = SUBLANES + seg_len
    band_stride = BAND if has_past else CHUNK

    @pl.when(t == 0)
    def _init():
        _expand_bias(brow_ref, bias_buf)
        if has_past:
            for i in range(n_seg):
                kbuf[i * BAND:i * BAND + BAND_PAST, :] = pk_ref[i].astype(BF16)
                vbuf[i * BAND:i * BAND + BAND_PAST, :] = pv_ref[i].astype(BF16)
                rxbuf[i * rx_stride + SUBLANES - tail:i * rx_stride + SUBLANES, :] = cr0_ref[i]
        else:
            kbuf[0:BAND_PAST, :] = jnp.zeros((BAND_PAST, D_ATTN), BF16)
            vbuf[0:BAND_PAST, :] = jnp.zeros((BAND_PAST, D_ATTN), BF16)
            rxbuf[0:SUBLANES, :] = jnp.zeros((SUBLANES, D_RNN), F32)
            hstate[...] = jnp.zeros((1, D_RNN), F32)

    x = x_ref[...]
    shift, scale, gate = (_per_row(mod_ref, r, n_seg, seg_len) for r in range(3))
    hb = _modulated_norm(x, g_ref[...], shift, scale).astype(BF16)

    k = _dot(hb, win_ref[:, C_K:C_K + D_ATTN])
    v = _dot(hb, win_ref[:, C_V:C_V + D_ATTN])
    if has_past:
        for i in range(n_seg):
            rows = slice(i * CHUNK, (i + 1) * CHUNK)
            kbuf[i * BAND + BAND_PAST:(i + 1) * BAND, :] = k[rows].astype(BF16)
            vbuf[i * BAND + BAND_PAST:(i + 1) * BAND, :] = v[rows].astype(BF16)
            ko_ref[i] = k[rows]
            vo_ref[i] = v[rows]
    else:
        kbuf[BAND_PAST:BAND_PAST + tm, :] = k.astype(BF16)
        vbuf[BAND_PAST:BAND_PAST + tm, :] = v.astype(BF16)
        first_keep = n_t - BAND_PAST // tm

        @pl.when(t >= first_keep)
        def _keep_kv():
            off = pl.multiple_of((t - first_keep) * tm, tm)
            ko_ref[0, pl.ds(off, tm), :] = k
            vo_ref[0, pl.ds(off, tm), :] = v

    q = _dot(hb, win_ref[:, C_Q:C_Q + D_ATTN]) * (HEAD_DIM ** -0.5)

    lane = lax.broadcasted_iota(jnp.int32, (CHUNK, LANES), 1)
    low = lane < HEAD_DIM
    units = [(j, p) for j in range(n_chunks) for p in range(N_HEAD_PAIRS)]

    def scores(j, p):
        cols = slice(LANES * p, LANES * (p + 1))
        qp = q[j * CHUNK:(j + 1) * CHUNK, cols]
        qs = jnp.concatenate(
            [jnp.where(low, qp, 0.0).astype(BF16), jnp.where(low, 0.0, qp).astype(BF16)], axis=0)
        kp = kbuf[j * band_stride:j * band_stride + BAND, cols]
        s = lax.dot_general(qs, kp, (((1,), (1,)), ((), ())), preferred_element_type=F32)
        s = s + bias_buf[p]
        if not has_past:
            col = lax.broadcasted_iota(jnp.int32, (PAIR_ROWS, BAND), 1)
            s = jnp.where(col >= BAND_PAST - CHUNK * (t * n_chunks + j), s, NEG_INF)
        return s

    def probs(s):
        m = jnp.max(s, axis=-1, keepdims=True)
        e = jnp.exp(s - m)
        return e.astype(BF16), jnp.sum(e, axis=-1, keepdims=True)

    def weighted(j, p, e, denom):
        cols = slice(LANES * p, LANES * (p + 1))
        o = _dot(e, vbuf[j * band_stride:j * band_stride + BAND, cols]) / denom
        attn_buf[j * CHUNK:(j + 1) * CHUNK, cols] = jnp.where(low, o[:CHUNK], o[CHUNK:]).astype(BF16)

    st = {}

    def piece_rx():
        rx = _dot(hb, win_ref[:, C_RX:C_RX + D_RNN])
        xc = []
        for i in range(n_seg):
            base = i * rx_stride + SUBLANES
            rxs = rx[i * seg_len:(i + 1) * seg_len]
            rxbuf[base:base + seg_len, :] = rxs
            xc.append(cb_ref[...] + (cw_ref[0:1, :] * rxbuf[base - 3:base - 3 + seg_len, :]
                                     + cw_ref[1:2, :] * rxbuf[base - 2:base - 2 + seg_len, :]
                                     + cw_ref[2:3, :] * rxbuf[base - 1:base - 1 + seg_len, :]
                                     + cw_ref[3:4, :] * rxs))
            if has_past:
                cro_ref[i] = rxbuf[base + seg_len - tail:base + seg_len, :]
        st["xc"] = jnp.concatenate(xc, axis=0)
        if not has_past:
            @pl.when(t == n_t - 1)
            def _keep_conv():
                cro_ref[0] = rxbuf[SUBLANES + tm - tail:SUBLANES + tm, :]

            rxbuf[0:SUBLANES, :] = rxbuf[tm:tm + SUBLANES, :]

    def piece_gates():
        xcb = st["xc"].astype(BF16)
        n_blk = D_RNN // RNN_GATE_BLK
        gates = [_dot(xcb[:, (i % n_blk) * RNN_GATE_BLK:(i % n_blk + 1) * RNN_GATE_BLK], gw_ref[i])
                 for i in range(2 * n_blk)]
        st["gates"] = jnp.concatenate(gates, axis=1) + gb_ref[...]
        st["carry"] = None if has_past else hstate[...]
        st["hs"] = []

    def piece_scan(blk):
        def run():
            rows = slice(blk * CHUNK, (blk + 1) * CHUNK)
            gates = st["gates"][rows, :]
            xc = st["xc"][rows, :]
            r = jax.nn.sigmoid(gates[:, :D_RNN])
            ig = jax.nn.sigmoid(gates[:, D_RNN:])
            log_a = -RG_C * r * jax.nn.softplus(-lam_ref[...])
            a = jnp.exp(log_a)
            u = jnp.sqrt(-jnp.tanh(log_a) * (a * a + 1.0)) * (ig * xc)
            a3, u3 = _scan_groups(a, u)
            carry = h0_ref[blk] if has_past else st["carry"]
            for i in range(CHUNK // SUBLANES):
                hi = u3[i] + a3[i] * carry
                st["hs"].append(hi)
                carry = hi[SUBLANES - 1:SUBLANES, :]
            if has_past:
                ho_ref[blk] = carry
            else:
                st["carry"] = carry
        return run

    def piece_rnn():
        if not has_past:
            h_last = st["carry"]
            hstate[...] = h_last

            @pl.when(t == n_t - 1)
            def _keep_h():
                ho_ref[0] = h_last

        rg = _dot(hb, win_ref[:, C_RG:C_RG + D_RNN])
        st["rnn"] = (jnp.concatenate(st["hs"], axis=0) * jax.nn.gelu(rg)).astype(BF16)

    def piece_gate_b(c):
        def run():
            cols = slice(c * GATE_COLS, (c + 1) * GATE_COLS)
            gb = _dot(hb, win_ref[:, C_GB + c * GATE_COLS:C_GB + (c + 1) * GATE_COLS])
            mrg_buf[:, cols] = jax.nn.sigmoid(gb) * _dot(st["rnn"], wbr_ref[D_ATTN:, cols])
        return run

    def piece_gate_a(c):
        def run():
            cols = slice(c * GATE_COLS, (c + 1) * GATE_COLS)
            ga = _dot(hb, win_ref[:, C_GA + c * GATE_COLS:C_GA + (c + 1) * GATE_COLS])
            sga_buf[:, cols] = jax.nn.sigmoid(ga)
        return run

    n_gate = D_MODEL // GATE_COLS
    pieces = [piece_rx, piece_gates] + [piece_scan(b) for b in range(n_chunks)] + [piece_rnn]
    pieces += [piece_gate_b(c) for c in range(n_gate)] + [piece_gate_a(c) for c in range(n_gate)]

    n_units = len(units)
    pending_s = [scores(*units[i]) for i in range(min(SCORES_AHEAD, n_units))]
    pending_e = [probs(pending_s.pop(0))]
    for i, (j, p) in enumerate(units):
        if pieces:
            pieces.pop(0)()
        weighted(j, p, *pending_e.pop(0))
        if pending_s:
            pending_e.append(probs(pending_s.pop(0)))
        if i + SCORES_AHEAD < n_units:
            pending_s.append(scores(*units[i + SCORES_AHEAD]))
    for piece in pieces:
        piece()

    if not has_past:
        for i in range(BAND_PAST // CHUNK):
            kbuf[i * CHUNK:(i + 1) * CHUNK, :] = kbuf[tm + i * CHUNK:tm + (i + 1) * CHUNK, :]
            vbuf[i * CHUNK:(i + 1) * CHUNK, :] = vbuf[tm + i * CHUNK:tm + (i + 1) * CHUNK, :]

    merged = []
    for c in range(n_gate):
        cols = slice(c * GATE_COLS, (c + 1) * GATE_COLS)
        ma = _dot(attn_buf[...], wbr_ref[0:D_ATTN, cols])
        merged.append((sga_buf[:, cols] * ma + mrg_buf[:, cols]).astype(BF16))
    merged = jnp.concatenate(merged, axis=1)
    xo_ref[...] = x + gate * _dot(merged, wout_ref[...])


def _mixer_layer(l, x, mod, mod_blk0, norm_g, w_in, bias_rows, conv_w, conv_b, gate_w, gate_b, lam,
                 w_branch, w_out, past):
    n_seq, n_tok, _ = x.shape
    has_past = past is not None
    n_seg, n_t = _tile_plan(n_seq, n_tok, has_past, MIXER_LONG_ROWS)
    tm = n_seg * n_tok // n_t
    n_grp = n_seq // n_seg
    keep_rows = min(BAND_PAST, n_tok)
    assert tm % CHUNK == 0 and (has_past or keep_rows % tm == 0)
    tail = CONV_W - 1
    n_gate_blk = 2 * (D_RNN // RNN_GATE_BLK)

    def layer(shape):
        return pl.BlockSpec((None,) + shape, lambda b, t: (l,) + (0,) * len(shape),
                            pipeline_mode=pl.Buffered(1))

    def per_group(shape):
        return pl.BlockSpec((n_seg,) + shape, lambda b, t: (b,) + (0,) * len(shape))

    def layer_group(shape):
        return pl.BlockSpec((None, n_seg) + shape, lambda b, t: (l, b) + (0,) * len(shape))

    in_specs = [
        pl.BlockSpec((None, tm, D_MODEL), lambda b, t: (b, t, 0)),
        pl.BlockSpec((None, n_seg, 3, D_MODEL), lambda b, t: (l, b + mod_blk0, 0, 0)),
        layer((1, D_MODEL)),
        layer((D_MODEL, D_IN)),
        layer((N_HEADS, BIAS_W)),
        layer((CONV_W, D_RNN)),
        layer((1, D_RNN)),
        layer((n_gate_blk, RNN_GATE_BLK, RNN_GATE_BLK)),
        layer((1, 2 * D_RNN)),
        layer((1, D_RNN)),
        layer((D_ATTN + D_RNN, D_MODEL)),
        layer((D_MODEL, D_MODEL)),
    ]
    args = [x.reshape(n_grp, n_seg * n_tok, D_MODEL), mod, norm_g, w_in, bias_rows, conv_w, conv_b, gate_w,
            gate_b, lam, w_branch, w_out]
    if has_past:
        in_specs += [layer_group((BAND_PAST, D_ATTN)), layer_group((BAND_PAST, D_ATTN)),
                     layer_group((tail, D_RNN)), layer_group((1, D_RNN))]
        args += list(past)
    out_specs = [
        pl.BlockSpec((None, tm, D_MODEL), lambda b, t: (b, t, 0)),
        per_group((keep_rows, D_ATTN)),
        per_group((keep_rows, D_ATTN)),
        per_group((tail, D_RNN)),
        per_group((1, D_RNN)),
    ]
    out_shape = [
        jax.ShapeDtypeStruct((n_grp, n_seg * n_tok, D_MODEL), F32),
        jax.ShapeDtypeStruct((n_seq, keep_rows, D_ATTN), F32),
        jax.ShapeDtypeStruct((n_seq, keep_rows, D_ATTN), F32),
        jax.ShapeDtypeStruct((n_seq, tail, D_RNN), F32),
        jax.ShapeDtypeStruct((n_seq, 1, D_RNN), F32),
    ]
    band_rows = n_seg * BAND if has_past else BAND_PAST + tm
    scratch = [
        pltpu.VMEM((band_rows, D_ATTN), BF16),
        pltpu.VMEM((band_rows, D_ATTN), BF16),
        pltpu.VMEM((n_seg * (SUBLANES + tm // n_seg), D_RNN), F32),
        pltpu.VMEM((1, D_RNN), F32),
        pltpu.VMEM((N_HEAD_PAIRS, PAIR_ROWS, BAND), F32),
        pltpu.VMEM((tm, D_ATTN), BF16),
        pltpu.VMEM((tm, D_MODEL), F32),
        pltpu.VMEM((tm, D_MODEL), F32),
    ]
    outs = pl.pallas_call(
        functools.partial(_mixer_kernel, tm, n_t, n_seg, has_past),
        grid=(n_grp, n_t),
        in_specs=in_specs,
        out_specs=out_specs,
        out_shape=out_shape,
        scratch_shapes=scratch,
        compiler_params=pltpu.CompilerParams(
            dimension_semantics=("parallel", "arbitrary"), vmem_limit_bytes=VMEM_LIMIT_BYTES),
        name=f"mixer_l{l}_{'sample' if has_past else 'prompt'}",
    )(*args)
    return (outs[0].reshape(n_seq, n_tok, D_MODEL),) + tuple(outs[1:])


def _ffn_kernel(tm, n_t, n_seg, has_past, final_norm, *refs):
    refs = list(refs)
    x_ref, mod_ref, g_ref, wup_ref, cw_ref, cb_ref, wdn_ref = refs[:7]
    refs = refs[7:]
    if final_norm:
        fg_ref = refs[0]
        refs = refs[1:]
    if has_past:
        cf0_ref = refs[0]
        refs = refs[1:]
    xo_ref, cfo_ref, hist = refs

    t = pl.program_id(1)
    seg_len = tm // n_seg
    tail = FFN_CONV_W - 1

    @pl.when(t == 0)
    def _init():
        hist[...] = jnp.zeros((n_seg * SUBLANES, 2 * D_FF), F32)
        if has_past:
            for i in range(n_seg):
                hist[(i + 1) * SUBLANES - tail:(i + 1) * SUBLANES, :] = cf0_ref[i]

    x = x_ref[...]
    shift, scale, gate = (_per_row(mod_ref, r, n_seg, seg_len) for r in range(3))
    hb = _modulated_norm(x, g_ref[...], shift, scale).astype(BF16)

    def project(c):
        return [_dot(hb, wup_ref[:, c0:c0 + FFN_COLS]) for c0 in (c * FFN_COLS, D_FF + c * FFN_COLS)]

    def conv_cols(c0, up):
        cols = slice(c0, c0 + FFN_COLS)
        out = []
        for i in range(n_seg):
            cur = up[i * seg_len:(i + 1) * seg_len]
            ext = jnp.concatenate([hist[i * SUBLANES:(i + 1) * SUBLANES, cols], cur], axis=0)
            back1 = pltpu.roll(ext, 1, axis=0)[SUBLANES:]
            back2 = pltpu.roll(ext, 2, axis=0)[SUBLANES:]
            out.append(cb_ref[:, cols] + (cw_ref[0:1, cols] * back2 + cw_ref[1:2, cols] * back1
                                          + cw_ref[2:3, cols] * cur))
            hist[i * SUBLANES:(i + 1) * SUBLANES, cols] = cur[seg_len - SUBLANES:]
        return jnp.concatenate(out, axis=0)

    n_chunks = D_FF // FFN_COLS
    acc = jnp.zeros((tm, D_MODEL), F32)
    ahead = [project(c) for c in range(min(UP_AHEAD, n_chunks))]
    for c in range(n_chunks):
        up_val, up_gate = ahead.pop(0)
        if c + UP_AHEAD < n_chunks:
            ahead.append(project(c + UP_AHEAD))
        val = conv_cols(c * FFN_COLS, up_val)
        gt = conv_cols(D_FF + c * FFN_COLS, up_gate)
        act = (val * jax.nn.gelu(gt)).astype(BF16)
        acc = acc + _dot(act, wdn_ref[c * FFN_COLS:(c + 1) * FFN_COLS, :])

    if has_past:
        for i in range(n_seg):
            cfo_ref[i] = hist[(i + 1) * SUBLANES - tail:(i + 1) * SUBLANES, :]
    else:
        @pl.when(t == n_t - 1)
        def _keep_conv():
            cfo_ref[0] = hist[SUBLANES - tail:SUBLANES, :]

    y = x + gate * acc
    if final_norm:
        ms = jnp.mean(y * y, axis=-1, keepdims=True)
        y = y * lax.rsqrt(ms + EPS) * fg_ref[...]
    xo_ref[...] = y


def _ffn_layer(l, x, mod, mod_blk0, norm_g, w_up, conv_w, conv_b, w_down, final_g, past):
    n_seq, n_tok, _ = x.shape
    has_past = past is not None
    n_seg, n_t = _tile_plan(n_seq, n_tok, has_past, FFN_LONG_ROWS)
    tm = n_seg * n_tok // n_t
    n_grp = n_seq // n_seg
    assert D_FF % FFN_COLS == 0
    final_norm = final_g is not None
    tail = FFN_CONV_W - 1

    def layer(shape):
        return pl.BlockSpec((None,) + shape, lambda b, t: (l,) + (0,) * len(shape),
                            pipeline_mode=pl.Buffered(1))

    in_specs = [
        pl.BlockSpec((None, tm, D_MODEL), lambda b, t: (b, t, 0)),
        pl.BlockSpec((None, n_seg, 3, D_MODEL), lambda b, t: (l, b + mod_blk0, 0, 0)),
        layer((1, D_MODEL)),
        layer((D_MODEL, 2 * D_FF)),
        layer((FFN_CONV_W, 2 * D_FF)),
        layer((1, 2 * D_FF)),
        layer((D_FF, D_MODEL)),
    ]
    args = [x.reshape(n_grp, n_seg * n_tok, D_MODEL), mod, norm_g, w_up, conv_w, conv_b, w_down]
    if final_norm:
        in_specs.append(pl.BlockSpec((1, D_MODEL), lambda b, t: (0, 0)))
        args.append(final_g)
    if has_past:
        in_specs.append(pl.BlockSpec((None, n_seg, tail, 2 * D_FF), lambda b, t: (l, b, 0, 0)))
        args.append(past)
    y, cf = pl.pallas_call(
        functools.partial(_ffn_kernel, tm, n_t, n_seg, has_past, final_norm),
        grid=(n_grp, n_t),
        in_specs=in_specs,
        out_specs=[
            pl.BlockSpec((None, tm, D_MODEL), lambda b, t: (b, t, 0)),
            pl.BlockSpec((n_seg, tail, 2 * D_FF), lambda b, t: (b, 0, 0)),
        ],
        out_shape=[
            jax.ShapeDtypeStruct((n_grp, n_seg * n_tok, D_MODEL), F32),
            jax.ShapeDtypeStruct((n_seq, tail, 2 * D_FF), F32),
        ],
        scratch_shapes=[pltpu.VMEM((n_seg * SUBLANES, 2 * D_FF), F32)],
        compiler_params=pltpu.CompilerParams(
            dimension_semantics=("parallel", "arbitrary"), vmem_limit_bytes=VMEM_LIMIT_BYTES),
        name=f"ffn_l{l}_{'sample' if has_past else 'prompt'}",
    )(*args)
    return y.reshape(n_seq, n_tok, D_MODEL), cf


def _bias_rows(table):
    idx = np.clip(BAND - np.arange(BIAS_W), -MAX_REL, MAX_REL) + MAX_REL
    return jnp.transpose(table[:, idx, :], (0, 2, 1))


def _block_diag(w, width):
    depth, n, i, j = w.shape
    per = width // i
    eye = jnp.eye(per, dtype=w.dtype)
    w = w.reshape(depth, n // per, per, i, j)
    return (w[:, :, :, :, None, :] * eye[None, None, :, None, :, None]).reshape(depth, n // per, width, width)


def kernel(x_prompt, x_sample, c_prompt, c_sample, cache_k, cache_v, state_rnn_conv, state_rnn_h,
           state_ffn_conv, mod_mix_w, mod_mix_b, norm_mix_g, w_in, rel_bias_table, rnn_conv_w,
           rnn_conv_b, rnn_gate_a_w, rnn_gate_a_b, rnn_gate_x_w, rnn_gate_x_b, rnn_lambda, w_branch,
           w_out, mod_ffn_w, mod_ffn_b, norm_ffn_g, ffn_up_w, ffn_conv_w, ffn_conv_b, ffn_down_w,
           final_norm_g):
    n_prompt = x_prompt.shape[0]
    n_sample = x_sample.shape[0]

    rows = n_sample + n_prompt
    pad = (-rows) % SUBLANES
    c_all = jnp.concatenate([c_sample, c_prompt, jnp.zeros((pad, D_MODEL), F32)], axis=0)
    mod_mix = _modulation(c_all, mod_mix_w, mod_mix_b)
    mod_ffn = _modulation(c_all, mod_ffn_w, mod_ffn_b)

    w_in_b = w_in.astype(BF16)
    w_branch_b = w_branch.astype(BF16)
    w_out_b = w_out.astype(BF16)
    w_up_b = ffn_up_w.astype(BF16)
    w_down_b = ffn_down_w.astype(BF16)
    gate_w = jnp.concatenate([_block_diag(rnn_gate_a_w, RNN_GATE_BLK), _block_diag(rnn_gate_x_w, RNN_GATE_BLK)],
                             axis=1).astype(BF16)
    gate_b = jnp.concatenate([rnn_gate_a_b, rnn_gate_x_b], axis=-1).reshape(DEPTH, 1, 2 * D_RNN)
    bias_rows = _bias_rows(rel_bias_table)
    norm_mix = norm_mix_g.reshape(DEPTH, 1, D_MODEL)
    norm_ffn = norm_ffn_g.reshape(DEPTH, 1, D_MODEL)
    conv_b = rnn_conv_b.reshape(DEPTH, 1, D_RNN)
    lam = rnn_lambda.reshape(DEPTH, 1, D_RNN)
    fconv_b = ffn_conv_b.reshape(DEPTH, 1, 2 * D_FF)
    final_g = final_norm_g.reshape(1, D_MODEL)

    past_k = cache_k.reshape(DEPTH, n_sample, BAND_PAST, D_ATTN).astype(BF16)
    past_v = cache_v.reshape(DEPTH, n_sample, BAND_PAST, D_ATTN).astype(BF16)
    past_h = state_rnn_h.reshape(DEPTH, n_sample, 1, D_RNN)

    def run(x, mod_blk0, with_past):
        ks, vs, crs, hs, cfs = [], [], [], [], []
        for l in range(DEPTH):
            past = (past_k, past_v, state_rnn_conv, past_h) if with_past else None
            x, k_new, v_new, cr_new, h_new = _mixer_layer(
                l, x, mod_mix, mod_blk0, norm_mix, w_in_b, bias_rows, rnn_conv_w, conv_b, gate_w, gate_b,
                lam, w_branch_b, w_out_b, past)
            x, cf_new = _ffn_layer(
                l, x, mod_ffn, mod_blk0, norm_ffn, w_up_b, ffn_conv_w, fconv_b, w_down_b,
                final_g if l == DEPTH - 1 else None, state_ffn_conv if with_past else None)
            ks.append(k_new.reshape(k_new.shape[0], k_new.shape[1], N_HEADS, HEAD_DIM))
            vs.append(v_new.reshape(v_new.shape[0], v_new.shape[1], N_HEADS, HEAD_DIM))
            crs.append(cr_new)
            hs.append(h_new[:, 0, :])
            cfs.append(cf_new)
        return x, jnp.stack(ks), jnp.stack(vs), jnp.stack(crs), jnp.stack(hs), jnp.stack(cfs)

    y_s, k_s, v_s, rc_s, h_s, fc_s = run(x_sample, 0, True)
    y_p, k_p, v_p, rc_p, h_p, fc_p = run(x_prompt, n_sample, False)
    return (y_p, y_s, k_p, v_p, k_s, v_s, rc_p, rc_s, h_p, h_s, fc_p, fc_s)
```

```python
import functools

import jax
import jax.numpy as jnp
import numpy as np
from jax import lax
from jax.experimental import pallas as pl
from jax.experimental.pallas import tpu as pltpu

F32 = jnp.float32
BF16 = jnp.bfloat16

D_MODEL = 1024
DEPTH = 2
CHUNK = 64
N_BAND_PAST = 8
BAND_PAST = N_BAND_PAST * CHUNK
BAND = BAND_PAST + CHUNK
N_HEADS = 8
HEAD_DIM = 64
D_ATTN = N_HEADS * HEAD_DIM
MAX_REL = 256
D_RNN = 512
N_RNN_BLOCKS = 8
CONV_W = 4
RG_C = 8.0
D_FF = 2816
FFN_CONV_W = 3
EPS = 1e-6
NEG_INF = -1e30

SUBLANES = 8
LANES = 128
MXU_TILE = 256
HEADS_PER_VREG = LANES // HEAD_DIM
N_HEAD_PAIRS = N_HEADS // HEADS_PER_VREG
PAIR_ROWS = HEADS_PER_VREG * CHUNK
BIAS_W = BAND + CHUNK
FFN_COLS = MXU_TILE
GATE_COLS = MXU_TILE
RNN_GATE_BLK = MXU_TILE
SCORES_AHEAD = 3
UP_AHEAD = 3
TILE_ROWS = 256
MIXER_LONG_ROWS = 512
FFN_LONG_ROWS = 256
VMEM_LIMIT_BYTES = 56 * 1024 * 1024
N_MIXER_STATE_OUT = 4

C_Q, C_K, C_V = 0, D_ATTN, 2 * D_ATTN
C_RX = 3 * D_ATTN
C_RG = C_RX + D_RNN
C_GA = C_RG + D_RNN
C_GB = C_GA + D_MODEL
D_IN = C_GB + D_MODEL


def _dot(a, b):
    return jnp.dot(a, b, preferred_element_type=F32)


def _modulated_norm(x, g, shift, scale):
    ms = jnp.mean(x * x, axis=-1, keepdims=True)
    y = x * lax.rsqrt(ms + EPS)
    return (y * g) * (1.0 + scale) + shift


def _per_row(mod_ref, r, n_seg, seg_len):
    if n_seg == 1:
        return mod_ref[0, r:r + 1, :]
    return jnp.concatenate(
        [jnp.broadcast_to(mod_ref[i, r:r + 1, :], (seg_len, D_MODEL)) for i in range(n_seg)], axis=0)


def _tile_plan(n_seq, n_tok, independent, long_rows):
    if independent:
        n_seg = max(1, min(n_seq, TILE_ROWS // n_tok))
        assert n_seq % n_seg == 0 and n_tok == CHUNK
        return n_seg, 1
    assert n_tok % long_rows == 0
    return 1, n_tok // long_rows


def _mod_kernel(c_ref, w_ref, b_ref, o_ref):
    c = c_ref[...]
    s = (c * jax.nn.sigmoid(c)).astype(BF16)
    o_ref[...] = _dot(s, w_ref[...].astype(BF16)) + b_ref[...]


def _modulation(c_all, w, b):
    rows = c_all.shape[0]
    n_col = (3 * D_MODEL) // D_MODEL
    out = pl.pallas_call(
        _mod_kernel,
        grid=(DEPTH, n_col),
        in_specs=[
            pl.BlockSpec((rows, D_MODEL), lambda l, j: (0, 0)),
            pl.BlockSpec((None, D_MODEL, D_MODEL), lambda l, j: (l, 0, j)),
            pl.BlockSpec((None, 1, D_MODEL), lambda l, j: (l, 0, j)),
        ],
        out_specs=pl.BlockSpec((None, rows, D_MODEL), lambda l, j: (l, 0, j)),
        out_shape=jax.ShapeDtypeStruct((DEPTH, rows, 3 * D_MODEL), F32),
        compiler_params=pltpu.CompilerParams(dimension_semantics=("parallel", "parallel")),
        name="adaln_mod",
    )(c_all, w, b.reshape(DEPTH, 1, 3 * D_MODEL))
    return out.reshape(DEPTH, rows, 3, D_MODEL)


def _expand_bias(g_ref, bias_buf):
    for p in range(N_HEAD_PAIRS):
        rows = []
        for hh in range(HEADS_PER_VREG):
            h = HEADS_PER_VREG * p + hh
            g = jnp.broadcast_to(g_ref[h:h + 1, :], (CHUNK, BIAS_W))
            rows.append(pltpu.roll(g, BAND, axis=1, stride=1, stride_axis=0)[:, :BAND])
        bias_buf[p] = jnp.concatenate(rows, axis=0)


def _scan_groups(a, u):
    nv = a.shape[0] // SUBLANES
    a3 = a.reshape(nv, SUBLANES, D_RNN)
    u3 = u.reshape(nv, SUBLANES, D_RNN)
    sub = lax.broadcasted_iota(jnp.int32, (nv, SUBLANES, D_RNN), 1)
    d = 1
    while d < SUBLANES:
        keep = sub >= d
        a_s = pltpu.roll(a3, d, axis=1)
        u_s = pltpu.roll(u3, d, axis=1)
        u3 = jnp.where(keep, a3 * u_s + u3, u3)
        a3 = jnp.where(keep, a3 * a_s, a3)
        d *= 2
    return a3, u3


def _mixer_kernel(tm, n_t, n_seg, has_past, *refs):
    refs = list(refs)
    (x_ref, mod_ref, g_ref, win_ref, brow_ref, cw_ref, cb_ref, gw_ref, gb_ref, lam_ref,
     wbr_ref, wout_ref) = refs[:12]
    refs = refs[12:]
    if has_past:
        pk_ref, pv_ref, cr0_ref, h0_ref = refs[:4]
        refs = refs[4:]
    refs = refs[N_MIXER_STATE_OUT:]
    (xo_ref, ko_ref, vo_ref, cro_ref, ho_ref,
     kbuf, vbuf, rxbuf, hstate, bias_buf, attn_buf, mrg_buf, sga_buf) = refs

    t = pl.program_id(1)
    seg_len = tm // n_seg
    n_chunks = tm // CHUNK
    tail = CONV_W - 1
    rx_stride = SUBLANES + seg_len
    band_stride = BAND if has_past else CHUNK

    @pl.when(t == 0)
    def _init():
        _expand_bias(brow_ref, bias_buf)
        if has_past:
            for i in range(n_seg):
                kbuf[i * BAND:i * BAND + BAND_PAST, :] = pk_ref[i].astype(BF16)
                vbuf[i * BAND:i * BAND + BAND_PAST, :] = pv_ref[i].astype(BF16)
                rxbuf[i * rx_stride + SUBLANES - tail:i * rx_stride + SUBLANES, :] = cr0_ref[i]
        else:
            kbuf[0:BAND_PAST, :] = jnp.zeros((BAND_PAST, D_ATTN), BF16)
            vbuf[0:BAND_PAST, :] = jnp.zeros((BAND_PAST, D_ATTN), BF16)
            rxbuf[0:SUBLANES, :] = jnp.zeros((SUBLANES, D_RNN), F32)
            hstate[...] = jnp.zeros((1, D_RNN), F32)

    x = x_ref[...]
    shift, scale, gate = (_per_row(mod_ref, r, n_seg, seg_len) for r in range(3))
    hb = _modulated_norm(x, g_ref[...], shift, scale).astype(BF16)

    k = _dot(hb, win_ref[:, C_K:C_K + D_ATTN])
    v = _dot(hb, win_ref[:, C_V:C_V + D_ATTN])
    if has_past:
        for i in range(n_seg):
            rows = slice(i * CHUNK, (i + 1) * CHUNK)
            kbuf[i * BAND + BAND_PAST:(i + 1) * BAND, :] = k[rows].astype(BF16)
            vbuf[i * BAND + BAND_PAST:(i + 1) * BAND, :] = v[rows].astype(BF16)
            ko_ref[i] = k[rows]
            vo_ref[i] = v[rows]
    else:
        kbuf[BAND_PAST:BAND_PAST + tm, :] = k.astype(BF16)
        vbuf[BAND_PAST:BAND_PAST + tm, :] = v.astype(BF16)
        first_keep = n_t - BAND_PAST // tm

        @pl.when(t >= first_keep)
        def _keep_kv():
            off = pl.multiple_of((t - first_keep) * tm, tm)
            ko_ref[0, pl.ds(off, tm), :] = k
            vo_ref[0, pl.ds(off, tm), :] = v

    q = _dot(hb, win_ref[:, C_Q:C_Q + D_ATTN]) * (HEAD_DIM ** -0.5)

    lane = lax.broadcasted_iota(jnp.int32, (CHUNK, LANES), 1)
    low = lane < HEAD_DIM
    units = [(j, p) for j in range(n_chunks) for p in range(N_HEAD_PAIRS)]

    def scores(j, p):
        cols = slice(LANES * p, LANES * (p + 1))
        qp = q[j * CHUNK:(j + 1) * CHUNK, cols]
        qs = jnp.concatenate(
            [jnp.where(low, qp, 0.0).astype(BF16), jnp.where(low, 0.0, qp).astype(BF16)], axis=0)
        kp = kbuf[j * band_stride:j * band_stride + BAND, cols]
        s = lax.dot_general(qs, kp, (((1,), (1,)), ((), ())), preferred_element_type=F32)
        s = s + bias_buf[p]
        if not has_past:
            col = lax.broadcasted_iota(jnp.int32, (PAIR_ROWS, BAND), 1)
            s = jnp.where(col >= BAND_PAST - CHUNK * (t * n_chunks + j), s, NEG_INF)
        return s

    def probs(s):
        m = jnp.max(s, axis=-1, keepdims=True)
        e = jnp.exp(s - m)
        return e.astype(BF16), jnp.sum(e, axis=-1, keepdims=True)

    def weighted(j, p, e, denom):
        cols = slice(LANES * p, LANES * (p + 1))
        o = _dot(e, vbuf[j * band_stride:j * band_stride + BAND, cols]) / denom
        attn_buf[j * CHUNK:(j + 1) * CHUNK, cols] = jnp.where(low, o[:CHUNK], o[CHUNK:]).astype(BF16)

    st = {}

    def piece_rx():
        rx = _dot(hb, win_ref[:, C_RX:C_RX + D_RNN])
        xc = []
        for i in range(n_seg):
            base = i * rx_stride + SUBLANES
            rxs = rx[i * seg_len:(i + 1) * seg_len]
            rxbuf[base:base + seg_len, :] = rxs
            xc.append(cb_ref[...] + (cw_ref[0:1, :] * rxbuf[base - 3:base - 3 + seg_len, :]
                                     + cw_ref[1:2, :] * rxbuf[base - 2:base - 2 + seg_len, :]
                                     + cw_ref[2:3, :] * rxbuf[base - 1:base - 1 + seg_len, :]
                                     + cw_ref[3:4, :] * rxs))
            if has_past:
                cro_ref[i] = rxbuf[base + seg_len - tail:base + seg_len, :]
        st["xc"] = jnp.concatenate(xc, axis=0)
        if not has_past:
            @pl.when(t == n_t - 1)
            def _keep_conv():
                cro_ref[0] = rxbuf[SUBLANES + tm - tail:SUBLANES + tm, :]

            rxbuf[0:SUBLANES, :] = rxbuf[tm:tm + SUBLANES, :]

    def piece_gates():
        xcb = st["xc"].astype(BF16)
        n_blk = D_RNN // RNN_GATE_BLK
        gates = [_dot(xcb[:, (i % n_blk) * RNN_GATE_BLK:(i % n_blk + 1) * RNN_GATE_BLK], gw_ref[i])
                 for i in range(2 * n_blk)]
        st["gates"] = jnp.concatenate(gates, axis=1) + gb_ref[...]
        st["carry"] = None if has_past else hstate[...]
        st["hs"] = []

    def piece_scan(blk):
        def run():
            rows = slice(blk * CHUNK, (blk + 1) * CHUNK)
            gates = st["gates"][rows, :]
            xc = st["xc"][rows, :]
            r = jax.nn.sigmoid(gates[:, :D_RNN])
            ig = jax.nn.sigmoid(gates[:, D_RNN:])
            log_a = -RG_C * r * jax.nn.softplus(-lam_ref[...])
            a = jnp.exp(log_a)
            u = jnp.sqrt(-jnp.tanh(log_a) * (a * a + 1.0)) * (ig * xc)
            a3, u3 = _scan_groups(a, u)
            carry = h0_ref[blk] if has_past else st["carry"]
            for i in range(CHUNK // SUBLANES):
                hi = u3[i] + a3[i] * carry
                st["hs"].append(hi)
                carry = hi[SUBLANES - 1:SUBLANES, :]
            if has_past:
                ho_ref[blk] = carry
            else:
                st["carry"] = carry
        return run

    def piece_rnn():
        if not has_past:
            h_last = st["carry"]
            hstate[...] = h_last

            @pl.when(t == n_t - 1)
            def _keep_h():
                ho_ref[0] = h_last

        rg = _dot(hb, win_ref[:, C_RG:C_RG + D_RNN])
        st["rnn"] = (jnp.concatenate(st["hs"], axis=0) * jax.nn.gelu(rg)).astype(BF16)

    def piece_gate_b(c):
        def run():
            cols = slice(c * GATE_COLS, (c + 1) * GATE_COLS)
            gb = _dot(hb, win_ref[:, C_GB + c * GATE_COLS:C_GB + (c + 1) * GATE_COLS])
            mrg_buf[:, cols] = jax.nn.sigmoid(gb)
        return run

    def piece_branch_b(c):
        def run():
            cols = slice(c * GATE_COLS, (c + 1) * GATE_COLS)
            mrg_buf[:, cols] = mrg_buf[:, cols] * _dot(st["rnn"], wbr_ref[D_ATTN:, cols])
        return run

    def piece_gate_a(c):
        def run():
            cols = slice(c * GATE_COLS, (c + 1) * GATE_COLS)
            ga = _dot(hb, win_ref[:, C_GA + c * GATE_COLS:C_GA + (c + 1) * GATE_COLS])
            sga_buf[:, cols] = jax.nn.sigmoid(ga)
        return run

    n_gate = D_MODEL // GATE_COLS
    pieces = [piece_rx, piece_gates]
    for i in range(max(n_chunks, n_gate)):
        pieces += [piece_scan(i)] if i < n_chunks else []
        pieces += [piece_gate_a(i)] if i < n_gate else []
    pieces += [piece_gate_b(c) for c in range(n_gate)] + [piece_rnn]
    pieces += [piece_branch_b(c) for c in range(n_gate)]

    n_units = len(units)
    pending_s = [scores(*units[i]) for i in range(min(SCORES_AHEAD, n_units))]
    pending_e = [probs(pending_s.pop(0))]
    for i, (j, p) in enumerate(units):
        if pieces:
            pieces.pop(0)()
        weighted(j, p, *pending_e.pop(0))
        if pending_s:
            pending_e.append(probs(pending_s.pop(0)))
        if i + SCORES_AHEAD < n_units:
            pending_s.append(scores(*units[i + SCORES_AHEAD]))
    for piece in pieces:
        piece()

    if not has_past:
        for i in range(BAND_PAST // CHUNK):
            kbuf[i * CHUNK:(i + 1) * CHUNK, :] = kbuf[tm + i * CHUNK:tm + (i + 1) * CHUNK, :]
            vbuf[i * CHUNK:(i + 1) * CHUNK, :] = vbuf[tm + i * CHUNK:tm + (i + 1) * CHUNK, :]

    merged = []
    for c in range(n_gate):
        cols = slice(c * GATE_COLS, (c + 1) * GATE_COLS)
        ma = _dot(attn_buf[...], wbr_ref[0:D_ATTN, cols])
        merged.append((sga_buf[:, cols] * ma + mrg_buf[:, cols]).astype(BF16))
    merged = jnp.concatenate(merged, axis=1)
    xo_ref[...] = x + gate * _dot(merged, wout_ref[...])


def _mixer_layer(l, x, mod, mod_blk0, norm_g, w_in, bias_rows, conv_w, conv_b, gate_w, gate_b, lam,
                 w_branch, w_out, past, state_out):
    n_seq, n_tok, _ = x.shape
    has_past = past is not None
    n_seg, n_t = _tile_plan(n_seq, n_tok, has_past, MIXER_LONG_ROWS)
    tm = n_seg * n_tok // n_t
    n_grp = n_seq // n_seg
    keep_rows = min(BAND_PAST, n_tok)
    assert tm % CHUNK == 0 and (has_past or keep_rows % tm == 0)
    tail = CONV_W - 1
    n_gate_blk = 2 * (D_RNN // RNN_GATE_BLK)

    def layer(shape):
        return pl.BlockSpec((None,) + shape, lambda b, t: (l,) + (0,) * len(shape),
                            pipeline_mode=pl.Buffered(1))

    def layer_group(shape):
        return pl.BlockSpec((None, n_seg) + shape, lambda b, t: (l, b) + (0,) * len(shape))

    in_specs = [
        pl.BlockSpec((None, tm, D_MODEL), lambda b, t: (b, t, 0)),
        pl.BlockSpec((None, n_seg, 3, D_MODEL), lambda b, t: (l, b + mod_blk0, 0, 0)),
        layer((1, D_MODEL)),
        layer((D_MODEL, D_IN)),
        layer((N_HEADS, BIAS_W)),
        layer((CONV_W, D_RNN)),
        layer((1, D_RNN)),
        layer((n_gate_blk, RNN_GATE_BLK, RNN_GATE_BLK)),
        layer((1, 2 * D_RNN)),
        layer((1, D_RNN)),
        layer((D_ATTN + D_RNN, D_MODEL)),
        layer((D_MODEL, D_MODEL)),
    ]
    args = [x.reshape(n_grp, n_seg * n_tok, D_MODEL), mod, norm_g, w_in, bias_rows, conv_w, conv_b, gate_w,
            gate_b, lam, w_branch, w_out]
    if has_past:
        in_specs += [layer_group((BAND_PAST, D_ATTN)), layer_group((BAND_PAST, D_ATTN)),
                     layer_group((tail, D_RNN)), layer_group((1, D_RNN))]
        args += list(past)
    assert len(state_out) == N_MIXER_STATE_OUT
    first_alias = len(args)
    in_specs += [pl.BlockSpec(memory_space=pl.ANY)] * N_MIXER_STATE_OUT
    args += list(state_out)
    out_specs = [
        pl.BlockSpec((None, tm, D_MODEL), lambda b, t: (b, t, 0)),
        layer_group((keep_rows, D_ATTN)),
        layer_group((keep_rows, D_ATTN)),
        layer_group((tail, D_RNN)),
        layer_group((1, D_RNN)),
    ]
    out_shape = [jax.ShapeDtypeStruct((n_grp, n_seg * n_tok, D_MODEL), F32)]
    out_shape += [jax.ShapeDtypeStruct(a.shape, a.dtype) for a in state_out]
    band_rows = n_seg * BAND if has_past else BAND_PAST + tm
    scratch = [
        pltpu.VMEM((band_rows, D_ATTN), BF16),
        pltpu.VMEM((band_rows, D_ATTN), BF16),
        pltpu.VMEM((n_seg * (SUBLANES + tm // n_seg), D_RNN), F32),
        pltpu.VMEM((1, D_RNN), F32),
        pltpu.VMEM((N_HEAD_PAIRS, PAIR_ROWS, BAND), F32),
        pltpu.VMEM((tm, D_ATTN), BF16),
        pltpu.VMEM((tm, D_MODEL), F32),
        pltpu.VMEM((tm, D_MODEL), F32),
    ]
    outs = pl.pallas_call(
        functools.partial(_mixer_kernel, tm, n_t, n_seg, has_past),
        grid=(n_grp, n_t),
        in_specs=in_specs,
        out_specs=out_specs,
        out_shape=out_shape,
        scratch_shapes=scratch,
        input_output_aliases={first_alias + i: 1 + i for i in range(N_MIXER_STATE_OUT)},
        compiler_params=pltpu.CompilerParams(
            dimension_semantics=("parallel", "arbitrary"), vmem_limit_bytes=VMEM_LIMIT_BYTES),
        name=f"mixer_l{l}_{'sample' if has_past else 'prompt'}",
    )(*args)
    return (outs[0].reshape(n_seq, n_tok, D_MODEL),) + tuple(outs[1:])


def _ffn_kernel(tm, n_t, n_seg, has_past, final_norm, *refs):
    refs = list(refs)
    x_ref, mod_ref, g_ref, wup_ref, cw_ref, cb_ref, wdn_ref = refs[:7]
    refs = refs[7:]
    if final_norm:
        fg_ref = refs[0]
        refs = refs[1:]
    if has_past:
        cf0_ref = refs[0]
        refs = refs[1:]
    refs = refs[1:]
    xo_ref, cfo_ref, hist = refs

    t = pl.program_id(1)
    seg_len = tm // n_seg
    tail = FFN_CONV_W - 1

    @pl.when(t == 0)
    def _init():
        hist[...] = jnp.zeros((n_seg * SUBLANES, 2 * D_FF), F32)
        if has_past:
            for i in range(n_seg):
                hist[(i + 1) * SUBLANES - tail:(i + 1) * SUBLANES, :] = cf0_ref[i]

    x = x_ref[...]
    shift, scale, gate = (_per_row(mod_ref, r, n_seg, seg_len) for r in range(3))
    hb = _modulated_norm(x, g_ref[...], shift, scale).astype(BF16)

    def project(c):
        return [_dot(hb, wup_ref[:, c0:c0 + FFN_COLS]) for c0 in (c * FFN_COLS, D_FF + c * FFN_COLS)]

    def conv_cols(c0, up):
        cols = slice(c0, c0 + FFN_COLS)
        out = []
        for i in range(n_seg):
            cur = up[i * seg_len:(i + 1) * seg_len]
            ext = jnp.concatenate([hist[i * SUBLANES:(i + 1) * SUBLANES, cols], cur], axis=0)
            back1 = pltpu.roll(ext, 1, axis=0)[SUBLANES:]
            back2 = pltpu.roll(ext, 2, axis=0)[SUBLANES:]
            out.append(cb_ref[:, cols] + (cw_ref[0:1, cols] * back2 + cw_ref[1:2, cols] * back1
                                          + cw_ref[2:3, cols] * cur))
            hist[i * SUBLANES:(i + 1) * SUBLANES, cols] = cur[seg_len - SUBLANES:]
        return jnp.concatenate(out, axis=0)

    n_chunks = D_FF // FFN_COLS
    acc = jnp.zeros((tm, D_MODEL), F32)
    ahead = [project(c) for c in range(min(UP_AHEAD, n_chunks))]
    for c in range(n_chunks):
        up_val, up_gate = ahead.pop(0)
        if c + UP_AHEAD < n_chunks:
            ahead.append(project(c + UP_AHEAD))
        val = conv_cols(c * FFN_COLS, up_val)
        gt = conv_cols(D_FF + c * FFN_COLS, up_gate)
        act = (val * jax.nn.gelu(gt)).astype(BF16)
        acc = acc + _dot(act, wdn_ref[c * FFN_COLS:(c + 1) * FFN_COLS, :])

    if has_past:
        for i in range(n_seg):
            cfo_ref[i] = hist[(i + 1) * SUBLANES - tail:(i + 1) * SUBLANES, :]
    else:
        @pl.when(t == n_t - 1)
        def _keep_conv():
            cfo_ref[0] = hist[SUBLANES - tail:SUBLANES, :]

    y = x + gate * acc
    if final_norm:
        ms = jnp.mean(y * y, axis=-1, keepdims=True)
        y = y * lax.rsqrt(ms + EPS) * fg_ref[...]
    xo_ref[...] = y


def _ffn_layer(l, x, mod, mod_blk0, norm_g, w_up, conv_w, conv_b, w_down, final_g, past, state_out):
    n_seq, n_tok, _ = x.shape
    has_past = past is not None
    n_seg, n_t = _tile_plan(n_seq, n_tok, has_past, FFN_LONG_ROWS)
    tm = n_seg * n_tok // n_t
    n_grp = n_seq // n_seg
    assert D_FF % FFN_COLS == 0
    final_norm = final_g is not None
    tail = FFN_CONV_W - 1

    def layer(shape):
        return pl.BlockSpec((None,) + shape, lambda b, t: (l,) + (0,) * len(shape),
                            pipeline_mode=pl.Buffered(1))

    in_specs = [
        pl.BlockSpec((None, tm, D_MODEL), lambda b, t: (b, t, 0)),
        pl.BlockSpec((None, n_seg, 3, D_MODEL), lambda b, t: (l, b + mod_blk0, 0, 0)),
        layer((1, D_MODEL)),
        layer((D_MODEL, 2 * D_FF)),
        layer((FFN_CONV_W, 2 * D_FF)),
        layer((1, 2 * D_FF)),
        layer((D_FF, D_MODEL)),
    ]
    args = [x.reshape(n_grp, n_seg * n_tok, D_MODEL), mod, norm_g, w_up, conv_w, conv_b, w_down]
    if final_norm:
        in_specs.append(pl.BlockSpec((1, D_MODEL), lambda b, t: (0, 0)))
        args.append(final_g)
    if has_past:
        in_specs.append(pl.BlockSpec((None, n_seg, tail, 2 * D_FF), lambda b, t: (l, b, 0, 0)))
        args.append(past)
    in_specs.append(pl.BlockSpec(memory_space=pl.ANY))
    args.append(state_out)
    y, cf = pl.pallas_call(
        functools.partial(_ffn_kernel, tm, n_t, n_seg, has_past, final_norm),
        grid=(n_grp, n_t),
        in_specs=in_specs,
        out_specs=[
            pl.BlockSpec((None, tm, D_MODEL), lambda b, t: (b, t, 0)),
            pl.BlockSpec((None, n_seg, tail, 2 * D_FF), lambda b, t: (l, b, 0, 0)),
        ],
        out_shape=[
            jax.ShapeDtypeStruct((n_grp, n_seg * n_tok, D_MODEL), F32),
            jax.ShapeDtypeStruct(state_out.shape, state_out.dtype),
        ],
        input_output_aliases={len(args) - 1: 1},
        scratch_shapes=[pltpu.VMEM((n_seg * SUBLANES, 2 * D_FF), F32)],
        compiler_params=pltpu.CompilerParams(
            dimension_semantics=("parallel", "arbitrary"), vmem_limit_bytes=VMEM_LIMIT_BYTES),
        name=f"ffn_l{l}_{'sample' if has_past else 'prompt'}",
    )(*args)
    return y.reshape(n_seq, n_tok, D_MODEL), cf


def _bias_rows(table):
    idx = np.clip(BAND - np.arange(BIAS_W), -MAX_REL, MAX_REL) + MAX_REL
    return jnp.transpose(table[:, idx, :], (0, 2, 1))


def _block_diag(w, width):
    depth, n, i, j = w.shape
    per = width // i
    eye = jnp.eye(per, dtype=w.dtype)
    w = w.reshape(depth, n // per, per, i, j)
    return (w[:, :, :, :, None, :] * eye[None, None, :, None, :, None]).reshape(depth, n // per, width, width)


def kernel(x_prompt, x_sample, c_prompt, c_sample, cache_k, cache_v, state_rnn_conv, state_rnn_h,
           state_ffn_conv, mod_mix_w, mod_mix_b, norm_mix_g, w_in, rel_bias_table, rnn_conv_w,
           rnn_conv_b, rnn_gate_a_w, rnn_gate_a_b, rnn_gate_x_w, rnn_gate_x_b, rnn_lambda, w_branch,
           w_out, mod_ffn_w, mod_ffn_b, norm_ffn_g, ffn_up_w, ffn_conv_w, ffn_conv_b, ffn_down_w,
           final_norm_g):
    n_prompt = x_prompt.shape[0]
    n_sample = x_sample.shape[0]

    rows = n_sample + n_prompt
    pad = (-rows) % SUBLANES
    c_all = jnp.concatenate([c_sample, c_prompt, jnp.zeros((pad, D_MODEL), F32)], axis=0)
    mod_mix = _modulation(c_all, mod_mix_w, mod_mix_b)
    mod_ffn = _modulation(c_all, mod_ffn_w, mod_ffn_b)

    w_in_b = w_in.astype(BF16)
    w_branch_b = w_branch.astype(BF16)
    w_out_b = w_out.astype(BF16)
    w_up_b = ffn_up_w.astype(BF16)
    w_down_b = ffn_down_w.astype(BF16)
    gate_w = jnp.concatenate([_block_diag(rnn_gate_a_w, RNN_GATE_BLK), _block_diag(rnn_gate_x_w, RNN_GATE_BLK)],
                             axis=1).astype(BF16)
    gate_b = jnp.concatenate([rnn_gate_a_b, rnn_gate_x_b], axis=-1).reshape(DEPTH, 1, 2 * D_RNN)
    bias_rows = _bias_rows(rel_bias_table)
    norm_mix = norm_mix_g.reshape(DEPTH, 1, D_MODEL)
    norm_ffn = norm_ffn_g.reshape(DEPTH, 1, D_MODEL)
    conv_b = rnn_conv_b.reshape(DEPTH, 1, D_RNN)
    lam = rnn_lambda.reshape(DEPTH, 1, D_RNN)
    fconv_b = ffn_conv_b.reshape(DEPTH, 1, 2 * D_FF)
    final_g = final_norm_g.reshape(1, D_MODEL)

    past_k = cache_k.reshape(DEPTH, n_sample, BAND_PAST, D_ATTN)
    past_v = cache_v.reshape(DEPTH, n_sample, BAND_PAST, D_ATTN)
    past_h = state_rnn_h.reshape(DEPTH, n_sample, 1, D_RNN)

    def run(x, mod_blk0, with_past):
        n_seq, n_tok, _ = x.shape
        keep_rows = min(BAND_PAST, n_tok)
        mixer_state = (jnp.zeros((DEPTH, n_seq, keep_rows, D_ATTN), F32),
                       jnp.zeros((DEPTH, n_seq, keep_rows, D_ATTN), F32),
                       jnp.zeros((DEPTH, n_seq, CONV_W - 1, D_RNN), F32),
                       jnp.zeros((DEPTH, n_seq, 1, D_RNN), F32))
        ffn_state = jnp.zeros((DEPTH, n_seq, FFN_CONV_W - 1, 2 * D_FF), F32)
        for l in range(DEPTH):
            past = (past_k, past_v, state_rnn_conv, past_h) if with_past else None
            x, *mixer_state = _mixer_layer(
                l, x, mod_mix, mod_blk0, norm_mix, w_in_b, bias_rows, rnn_conv_w, conv_b, gate_w, gate_b,
                lam, w_branch_b, w_out_b, past, mixer_state)
            x, ffn_state = _ffn_layer(
                l, x, mod_ffn, mod_blk0, norm_ffn, w_up_b, ffn_conv_w, fconv_b, w_down_b,
                final_g if l == DEPTH - 1 else None, state_ffn_conv if with_past else None, ffn_state)
        k_all, v_all, cr_all, h_all = mixer_state
        heads = (DEPTH, n_seq, keep_rows, N_HEADS, HEAD_DIM)
        return x, k_all.reshape(heads), v_all.reshape(heads), cr_all, h_all[:, :, 0, :], ffn_state

    y_s, k_s, v_s, rc_s, h_s, fc_s = run(x_sample, 0, True)
    y_p, k_p, v_p, rc_p, h_p, fc_p = run(x_prompt, n_sample, False)
    return (y_p, y_s, k_p, v_p, k_s, v_s, rc_p, rc_s, h_p, h_s, fc_p, fc_s)
```

```python
import functools

import jax
import jax.numpy as jnp
import numpy as np
from jax import lax
from jax.experimental import pallas as pl
from jax.experimental.pallas import tpu as pltpu

F32 = jnp.float32
BF16 = jnp.bfloat16

D_MODEL = 1024
DEPTH = 2
CHUNK = 64
N_BAND_PAST = 8
BAND_PAST = N_BAND_PAST * CHUNK
BAND = BAND_PAST + CHUNK
N_HEADS = 8
HEAD_DIM = 64
D_ATTN = N_HEADS * HEAD_DIM
MAX_REL = 256
D_RNN = 512
N_RNN_BLOCKS = 8
CONV_W = 4
RG_C = 8.0
D_FF = 2816
FFN_CONV_W = 3
EPS = 1e-6
NEG_INF = -1e30

SUBLANES = 8
LANES = 128
MXU_TILE = 256
HEADS_PER_VREG = LANES // HEAD_DIM
N_HEAD_PAIRS = N_HEADS // HEADS_PER_VREG
PAIR_ROWS = HEADS_PER_VREG * CHUNK
BIAS_W = BAND + CHUNK
FFN_COLS = MXU_TILE
GATE_COLS = MXU_TILE
RNN_GATE_BLK = MXU_TILE
SCORES_AHEAD = 3
UP_AHEAD = 3
TILE_ROWS = 128
MIXER_LONG_ROWS = 512
FFN_LONG_ROWS = 256
VMEM_LIMIT_BYTES = 56 * 1024 * 1024
N_MIXER_STATE_OUT = 4

C_Q, C_K, C_V = 0, D_ATTN, 2 * D_ATTN
C_RX = 3 * D_ATTN
C_RG = C_RX + D_RNN
C_GA = C_RG + D_RNN
C_GB = C_GA + D_MODEL
D_IN = C_GB + D_MODEL


def _dot(a, b):
    return jnp.dot(a, b, preferred_element_type=F32)


def _modulated_norm(x, g, shift, scale):
    ms = jnp.mean(x * x, axis=-1, keepdims=True)
    y = x * lax.rsqrt(ms + EPS)
    return (y * g) * (1.0 + scale) + shift


def _per_row(mod_ref, r, n_seg, seg_len):
    if n_seg == 1:
        return mod_ref[0, r:r + 1, :]
    return jnp.concatenate(
        [jnp.broadcast_to(mod_ref[i, r:r + 1, :], (seg_len, D_MODEL)) for i in range(n_seg)], axis=0)


def _tile_plan(n_seq, n_tok, independent, long_rows):
    if independent:
        n_seg = max(1, min(n_seq, TILE_ROWS // n_tok))
        assert n_seq % n_seg == 0 and n_tok == CHUNK
        return n_seg, 1
    assert n_tok % long_rows == 0
    return 1, n_tok // long_rows


def _mod_kernel(c_ref, w_ref, b_ref, o_ref):
    c = c_ref[...]
    s = (c * jax.nn.sigmoid(c)).astype(BF16)
    o_ref[...] = _dot(s, w_ref[...].astype(BF16)) + b_ref[...]


def _modulation(c_all, w, b):
    rows = c_all.shape[0]
    n_col = (3 * D_MODEL) // D_MODEL
    out = pl.pallas_call(
        _mod_kernel,
        grid=(DEPTH, n_col),
        in_specs=[
            pl.BlockSpec((rows, D_MODEL), lambda l, j: (0, 0)),
            pl.BlockSpec((None, D_MODEL, D_MODEL), lambda l, j: (l, 0, j)),
            pl.BlockSpec((None, 1, D_MODEL), lambda l, j: (l, 0, j)),
        ],
        out_specs=pl.BlockSpec((None, rows, D_MODEL), lambda l, j: (l, 0, j)),
        out_shape=jax.ShapeDtypeStruct((DEPTH, rows, 3 * D_MODEL), F32),
        compiler_params=pltpu.CompilerParams(dimension_semantics=("parallel", "parallel")),
        name="adaln_mod",
    )(c_all, w, b.reshape(DEPTH, 1, 3 * D_MODEL))
    return out.reshape(DEPTH, rows, 3, D_MODEL)


def _expand_bias(g_ref, bias_buf):
    for p in range(N_HEAD_PAIRS):
        rows = []
        for hh in range(HEADS_PER_VREG):
            h = HEADS_PER_VREG * p + hh
            g = jnp.broadcast_to(g_ref[h:h + 1, :], (CHUNK, BIAS_W))
            rows.append(pltpu.roll(g, BAND, axis=1, stride=1, stride_axis=0)[:, :BAND])
        bias_buf[p] = jnp.concatenate(rows, axis=0)


def _scan_groups(a, u):
    nv = a.shape[0] // SUBLANES
    a3 = a.reshape(nv, SUBLANES, D_RNN)
    u3 = u.reshape(nv, SUBLANES, D_RNN)
    sub = lax.broadcasted_iota(jnp.int32, (nv, SUBLANES, D_RNN), 1)
    d = 1
    while d < SUBLANES:
        keep = sub >= d
        a_s = pltpu.roll(a3, d, axis=1)
        u_s = pltpu.roll(u3, d, axis=1)
        u3 = jnp.where(keep, a3 * u_s + u3, u3)
        a3 = jnp.where(keep, a3 * a_s, a3)
        d *= 2
    return a3, u3


def _mixer_kernel(tm, n_t, n_seg, has_past, *refs):
    refs = list(refs)
    (x_ref, mod_ref, g_ref, win_ref, brow_ref, cw_ref, cb_ref, gw_ref, gb_ref, lam_ref,
     wbr_ref, wout_ref) = refs[:12]
    refs = refs[12:]
    if has_past:
        pk_ref, pv_ref, cr0_ref, h0_ref = refs[:4]
        refs = refs[4:]
    refs = refs[N_MIXER_STATE_OUT:]
    (xo_ref, ko_ref, vo_ref, cro_ref, ho_ref,
     kbuf, vbuf, rxbuf, hstate, bias_buf, attn_buf, mrg_buf, sga_buf) = refs

    t = pl.program_id(1)
    seg_len = tm // n_seg
    n_chunks = tm // CHUNK
    tail = CONV_W - 1
    rx_stride = SUBLANES + seg_len
    band_stride = BAND if has_past else CHUNK

    @pl.when(t == 0)
    def _init():
        _expand_bias(brow_ref, bias_buf)
        if has_past:
            for i in range(n_seg):
                kbuf[i * BAND:i * BAND + BAND_PAST, :] = pk_ref[i].astype(BF16)
                vbuf[i * BAND:i * BAND + BAND_PAST, :] = pv_ref[i].astype(BF16)
                rxbuf[i * rx_stride + SUBLANES - tail:i * rx_stride + SUBLANES, :] = cr0_ref[i]
        else:
            kbuf[0:BAND_PAST, :] = jnp.zeros((BAND_PAST, D_ATTN), BF16)
            vbuf[0:BAND_PAST, :] = jnp.zeros((BAND_PAST, D_ATTN), BF16)
            rxbuf[0:SUBLANES, :] = jnp.zeros((SUBLANES, D_RNN), F32)
            hstate[...] = jnp.zeros((1, D_RNN), F32)

    x = x_ref[...]
    shift, scale, gate = (_per_row(mod_ref, r, n_seg, seg_len) for r in range(3))
    hb = _modulated_norm(x, g_ref[...], shift, scale).astype(BF16)

    k = _dot(hb, win_ref[:, C_K:C_K + D_ATTN])
    v = _dot(hb, win_ref[:, C_V:C_V + D_ATTN])
    if has_past:
        for i in range(n_seg):
            rows = slice(i * CHUNK, (i + 1) * CHUNK)
            kbuf[i * BAND + BAND_PAST:(i + 1) * BAND, :] = k[rows].astype(BF16)
            vbuf[i * BAND + BAND_PAST:(i + 1) * BAND, :] = v[rows].astype(BF16)
            ko_ref[i] = k[rows]
            vo_ref[i] = v[rows]
    else:
        kbuf[BAND_PAST:BAND_PAST + tm, :] = k.astype(BF16)
        vbuf[BAND_PAST:BAND_PAST + tm, :] = v.astype(BF16)
        first_keep = n_t - BAND_PAST // tm

        @pl.when(t >= first_keep)
        def _keep_kv():
            off = pl.multiple_of((t - first_keep) * tm, tm)
            ko_ref[0, pl.ds(off, tm), :] = k
            vo_ref[0, pl.ds(off, tm), :] = v

    q = _dot(hb, win_ref[:, C_Q:C_Q + D_ATTN]) * (HEAD_DIM ** -0.5)

    lane = lax.broadcasted_iota(jnp.int32, (CHUNK, LANES), 1)
    low = lane < HEAD_DIM
    units = [(j, p) for j in range(n_chunks) for p in range(N_HEAD_PAIRS)]

    def scores(j, p):
        cols = slice(LANES * p, LANES * (p + 1))
        qp = q[j * CHUNK:(j + 1) * CHUNK, cols]
        qs = jnp.concatenate(
            [jnp.where(low, qp, 0.0).astype(BF16), jnp.where(low, 0.0, qp).astype(BF16)], axis=0)
        kp = kbuf[j * band_stride:j * band_stride + BAND, cols]
        s = lax.dot_general(qs, kp, (((1,), (1,)), ((), ())), preferred_element_type=F32)
        s = s + bias_buf[p]
        if not has_past:
            col = lax.broadcasted_iota(jnp.int32, (PAIR_ROWS, BAND), 1)
            s = jnp.where(col >= BAND_PAST - CHUNK * (t * n_chunks + j), s, NEG_INF)
        return s

    def probs(s):
        m = jnp.max(s, axis=-1, keepdims=True)
        e = jnp.exp(s - m)
        return e.astype(BF16), jnp.sum(e, axis=-1, keepdims=True)

    def weighted(j, p, e, denom):
        cols = slice(LANES * p, LANES * (p + 1))
        o = _dot(e, vbuf[j * band_stride:j * band_stride + BAND, cols]) / denom
        attn_buf[j * CHUNK:(j + 1) * CHUNK, cols] = jnp.where(low, o[:CHUNK], o[CHUNK:]).astype(BF16)

    st = {}

    def piece_rx():
        rx = _dot(hb, win_ref[:, C_RX:C_RX + D_RNN])
        xc = []
        for i in range(n_seg):
            base = i * rx_stride + SUBLANES
            rxs = rx[i * seg_len:(i + 1) * seg_len]
            rxbuf[base:base + seg_len, :] = rxs
            xc.append(cb_ref[...] + (cw_ref[0:1, :] * rxbuf[base - 3:base - 3 + seg_len, :]
                                     + cw_ref[1:2, :] * rxbuf[base - 2:base - 2 + seg_len, :]
                                     + cw_ref[2:3, :] * rxbuf[base - 1:base - 1 + seg_len, :]
                                     + cw_ref[3:4, :] * rxs))
            if has_past:
                cro_ref[i] = rxbuf[base + seg_len - tail:base + seg_len, :]
        st["xc"] = jnp.concatenate(xc, axis=0)
        if not has_past:
            @pl.when(t == n_t - 1)
            def _keep_conv():
                cro_ref[0] = rxbuf[SUBLANES + tm - tail:SUBLANES + tm, :]

            rxbuf[0:SUBLANES, :] = rxbuf[tm:tm + SUBLANES, :]

    def piece_gates():
        xcb = st["xc"].astype(BF16)
        n_blk = D_RNN // RNN_GATE_BLK
        gates = [_dot(xcb[:, (i % n_blk) * RNN_GATE_BLK:(i % n_blk + 1) * RNN_GATE_BLK], gw_ref[i])
                 for i in range(2 * n_blk)]
        st["gates"] = jnp.concatenate(gates, axis=1) + gb_ref[...]
        st["carry"] = None if has_past else hstate[...]
        st["hs"] = []

    def piece_scan(blk):
        def run():
            rows = slice(blk * CHUNK, (blk + 1) * CHUNK)
            gates = st["gates"][rows, :]
            xc = st["xc"][rows, :]
            r = jax.nn.sigmoid(gates[:, :D_RNN])
            ig = jax.nn.sigmoid(gates[:, D_RNN:])
            log_a = -RG_C * r * jax.nn.softplus(-lam_ref[...])
            a = jnp.exp(log_a)
            u = jnp.sqrt(-jnp.tanh(log_a) * (a * a + 1.0)) * (ig * xc)
            a3, u3 = _scan_groups(a, u)
            carry = h0_ref[blk] if has_past else st["carry"]
            for i in range(CHUNK // SUBLANES):
                hi = u3[i] + a3[i] * carry
                st["hs"].append(hi)
                carry = hi[SUBLANES - 1:SUBLANES, :]
            if has_past:
                ho_ref[blk] = carry
            else:
                st["carry"] = carry
        return run

    def piece_rnn():
        if not has_past:
            h_last = st["carry"]
            hstate[...] = h_last

            @pl.when(t == n_t - 1)
            def _keep_h():
                ho_ref[0] = h_last

        rg = _dot(hb, win_ref[:, C_RG:C_RG + D_RNN])
        st["rnn"] = (jnp.concatenate(st["hs"], axis=0) * jax.nn.gelu(rg)).astype(BF16)

    def piece_gate_b(c):
        def run():
            cols = slice(c * GATE_COLS, (c + 1) * GATE_COLS)
            gb = _dot(hb, win_ref[:, C_GB + c * GATE_COLS:C_GB + (c + 1) * GATE_COLS])
            mrg_buf[:, cols] = jax.nn.sigmoid(gb)
        return run

    def piece_branch_b(c):
        def run():
            cols = slice(c * GATE_COLS, (c + 1) * GATE_COLS)
            mrg_buf[:, cols] = mrg_buf[:, cols] * _dot(st["rnn"], wbr_ref[D_ATTN:, cols])
        return run

    def piece_gate_a(c):
        def run():
            cols = slice(c * GATE_COLS, (c + 1) * GATE_COLS)
            ga = _dot(hb, win_ref[:, C_GA + c * GATE_COLS:C_GA + (c + 1) * GATE_COLS])
            sga_buf[:, cols] = jax.nn.sigmoid(ga)
        return run

    n_gate = D_MODEL // GATE_COLS
    pieces = [piece_rx, piece_gates]
    for i in range(max(n_chunks, n_gate)):
        pieces += [piece_scan(i)] if i < n_chunks else []
        pieces += [piece_gate_a(i)] if i < n_gate else []
    pieces += [piece_gate_b(c) for c in range(n_gate)] + [piece_rnn]
    pieces += [piece_branch_b(c) for c in range(n_gate)]

    n_units = len(units)
    pending_s = [scores(*units[i]) for i in range(min(SCORES_AHEAD, n_units))]
    pending_e = [probs(pending_s.pop(0))]
    for i, (j, p) in enumerate(units):
        if pieces:
            pieces.pop(0)()
        weighted(j, p, *pending_e.pop(0))
        if pending_s:
            pending_e.append(probs(pending_s.pop(0)))
        if i + SCORES_AHEAD < n_units:
            pending_s.append(scores(*units[i + SCORES_AHEAD]))
    for piece in pieces:
        piece()

    if not has_past:
        for i in range(BAND_PAST // CHUNK):
            kbuf[i * CHUNK:(i + 1) * CHUNK, :] = kbuf[tm + i * CHUNK:tm + (i + 1) * CHUNK, :]
            vbuf[i * CHUNK:(i + 1) * CHUNK, :] = vbuf[tm + i * CHUNK:tm + (i + 1) * CHUNK, :]

    merged = []
    for c in range(n_gate):
        cols = slice(c * GATE_COLS, (c + 1) * GATE_COLS)
        ma = _dot(attn_buf[...], wbr_ref[0:D_ATTN, cols])
        merged.append((sga_buf[:, cols] * ma + mrg_buf[:, cols]).astype(BF16))
    merged = jnp.concatenate(merged, axis=1)
    xo_ref[...] = x + gate * _dot(merged, wout_ref[...])


def _mixer_layer(l, x, mod, mod_blk0, norm_g, w_in, bias_rows, conv_w, conv_b, gate_w, gate_b, lam,
                 w_branch, w_out, past, state_out):
    n_seq, n_tok, _ = x.shape
    has_past = past is not None
    n_seg, n_t = _tile_plan(n_seq, n_tok, has_past, MIXER_LONG_ROWS)
    tm = n_seg * n_tok // n_t
    n_grp = n_seq // n_seg
    keep_rows = min(BAND_PAST, n_tok)
    assert tm % CHUNK == 0 and (has_past or keep_rows % tm == 0)
    tail = CONV_W - 1
    n_gate_blk = 2 * (D_RNN // RNN_GATE_BLK)

    def layer(shape):
        return pl.BlockSpec((None,) + shape, lambda b, t: (l,) + (0,) * len(shape),
                            pipeline_mode=pl.Buffered(1))

    def layer_group(shape):
        return pl.BlockSpec((None, n_seg) + shape, lambda b, t: (l, b) + (0,) * len(shape))

    in_specs = [
        pl.BlockSpec((None, tm, D_MODEL), lambda b, t: (b, t, 0)),
        pl.BlockSpec((None, n_seg, 3, D_MODEL), lambda b, t: (l, b + mod_blk0, 0, 0)),
        layer((1, D_MODEL)),
        layer((D_MODEL, D_IN)),
        layer((N_HEADS, BIAS_W)),
        layer((CONV_W, D_RNN)),
        layer((1, D_RNN)),
        layer((n_gate_blk, RNN_GATE_BLK, RNN_GATE_BLK)),
        layer((1, 2 * D_RNN)),
        layer((1, D_RNN)),
        layer((D_ATTN + D_RNN, D_MODEL)),
        layer((D_MODEL, D_MODEL)),
    ]
    args = [x.reshape(n_grp, n_seg * n_tok, D_MODEL), mod, norm_g, w_in, bias_rows, conv_w, conv_b, gate_w,
            gate_b, lam, w_branch, w_out]
    if has_past:
        in_specs += [layer_group((BAND_PAST, D_ATTN)), layer_group((BAND_PAST, D_ATTN)),
                     layer_group((tail, D_RNN)), layer_group((1, D_RNN))]
        args += list(past)
    assert len(state_out) == N_MIXER_STATE_OUT
    first_alias = len(args)
    in_specs += [pl.BlockSpec(memory_space=pl.ANY)] * N_MIXER_STATE_OUT
    args += list(state_out)
    out_specs = [
        pl.BlockSpec((None, tm, D_MODEL), lambda b, t: (b, t, 0)),
        layer_group((keep_rows, D_ATTN)),
        layer_group((keep_rows, D_ATTN)),
        layer_group((tail, D_RNN)),
        layer_group((1, D_RNN)),
    ]
    out_shape = [jax.ShapeDtypeStruct((n_grp, n_seg * n_tok, D_MODEL), F32)]
    out_shape += [jax.ShapeDtypeStruct(a.shape, a.dtype) for a in state_out]
    band_rows = n_seg * BAND if has_past else BAND_PAST + tm
    scratch = [
        pltpu.VMEM((band_rows, D_ATTN), BF16),
        pltpu.VMEM((band_rows, D_ATTN), BF16),
        pltpu.VMEM((n_seg * (SUBLANES + tm // n_seg), D_RNN), F32),
        pltpu.VMEM((1, D_RNN), F32),
        pltpu.VMEM((N_HEAD_PAIRS, PAIR_ROWS, BAND), F32),
        pltpu.VMEM((tm, D_ATTN), BF16),
        pltpu.VMEM((tm, D_MODEL), F32),
        pltpu.VMEM((tm, D_MODEL), F32),
    ]
    outs = pl.pallas_call(
        functools.partial(_mixer_kernel, tm, n_t, n_seg, has_past),
        grid=(n_grp, n_t),
        in_specs=in_specs,
        out_specs=out_specs,
        out_shape=out_shape,
        scratch_shapes=scratch,
        input_output_aliases={first_alias + i: 1 + i for i in range(N_MIXER_STATE_OUT)},
        compiler_params=pltpu.CompilerParams(
            dimension_semantics=("parallel", "arbitrary"), vmem_limit_bytes=VMEM_LIMIT_BYTES),
        name=f"mixer_l{l}_{'sample' if has_past else 'prompt'}",
    )(*args)
    return (outs[0].reshape(n_seq, n_tok, D_MODEL),) + tuple(outs[1:])


def _ffn_kernel(tm, n_t, n_seg, has_past, final_norm, *refs):
    refs = list(refs)
    x_ref, mod_ref, g_ref, wup_ref, cw_ref, cb_ref, wdn_ref = refs[:7]
    refs = refs[7:]
    if final_norm:
        fg_ref = refs[0]
        refs = refs[1:]
    if has_past:
        cf0_ref = refs[0]
        refs = refs[1:]
    refs = refs[1:]
    xo_ref, cfo_ref, hist = refs

    t = pl.program_id(1)
    seg_len = tm // n_seg
    tail = FFN_CONV_W - 1

    @pl.when(t == 0)
    def _init():
        hist[...] = jnp.zeros((n_seg * SUBLANES, 2 * D_FF), F32)
        if has_past:
            for i in range(n_seg):
                hist[(i + 1) * SUBLANES - tail:(i + 1) * SUBLANES, :] = cf0_ref[i]

    x = x_ref[...]
    shift, scale, gate = (_per_row(mod_ref, r, n_seg, seg_len) for r in range(3))
    hb = _modulated_norm(x, g_ref[...], shift, scale).astype(BF16)

    def project(c):
        return [_dot(hb, wup_ref[:, c0:c0 + FFN_COLS]) for c0 in (c * FFN_COLS, D_FF + c * FFN_COLS)]

    def conv_cols(c0, up):
        cols = slice(c0, c0 + FFN_COLS)
        out = []
        for i in range(n_seg):
            cur = up[i * seg_len:(i + 1) * seg_len]
            ext = jnp.concatenate([hist[i * SUBLANES:(i + 1) * SUBLANES, cols], cur], axis=0)
            back1 = pltpu.roll(ext, 1, axis=0)[SUBLANES:]
            back2 = pltpu.roll(ext, 2, axis=0)[SUBLANES:]
            out.append(cb_ref[:, cols] + (cw_ref[0:1, cols] * back2 + cw_ref[1:2, cols] * back1
                                          + cw_ref[2:3, cols] * cur))
            hist[i * SUBLANES:(i + 1) * SUBLANES, cols] = cur[seg_len - SUBLANES:]
        return jnp.concatenate(out, axis=0)

    n_chunks = D_FF // FFN_COLS
    acc = jnp.zeros((tm, D_MODEL), F32)
    ahead = [project(c) for c in range(min(UP_AHEAD, n_chunks))]
    for c in range(n_chunks):
        up_val, up_gate = ahead.pop(0)
        if c + UP_AHEAD < n_chunks:
            ahead.append(project(c + UP_AHEAD))
        val = conv_cols(c * FFN_COLS, up_val)
        gt = conv_cols(D_FF + c * FFN_COLS, up_gate)
        act = (val * jax.nn.gelu(gt)).astype(BF16)
        acc = acc + _dot(act, wdn_ref[c * FFN_COLS:(c + 1) * FFN_COLS, :])

    if has_past:
        for i in range(n_seg):
            cfo_ref[i] = hist[(i + 1) * SUBLANES - tail:(i + 1) * SUBLANES, :]
    else:
        @pl.when(t == n_t - 1)
        def _keep_conv():
            cfo_ref[0] = hist[SUBLANES - tail:SUBLANES, :]

    y = x + gate * acc
    if final_norm:
        ms = jnp.mean(y * y, axis=-1, keepdims=True)
        y = y * lax.rsqrt(ms + EPS) * fg_ref[...]
    xo_ref[...] = y


def _ffn_layer(l, x, mod, mod_blk0, norm_g, w_up, conv_w, conv_b, w_down, final_g, past, state_out):
    n_seq, n_tok, _ = x.shape
    has_past = past is not None
    n_seg, n_t = _tile_plan(n_seq, n_tok, has_past, FFN_LONG_ROWS)
    tm = n_seg * n_tok // n_t
    n_grp = n_seq // n_seg
    assert D_FF % FFN_COLS == 0
    final_norm = final_g is not None
    tail = FFN_CONV_W - 1

    def layer(shape):
        return pl.BlockSpec((None,) + shape, lambda b, t: (l,) + (0,) * len(shape),
                            pipeline_mode=pl.Buffered(1))

    in_specs = [
        pl.BlockSpec((None, tm, D_MODEL), lambda b, t: (b, t, 0)),
        pl.BlockSpec((None, n_seg, 3, D_MODEL), lambda b, t: (l, b + mod_blk0, 0, 0)),
        layer((1, D_MODEL)),
        layer((D_MODEL, 2 * D_FF)),
        layer((FFN_CONV_W, 2 * D_FF)),
        layer((1, 2 * D_FF)),
        layer((D_FF, D_MODEL)),
    ]
    args = [x.reshape(n_grp, n_seg * n_tok, D_MODEL), mod, norm_g, w_up, conv_w, conv_b, w_down]
    if final_norm:
        in_specs.append(pl.BlockSpec((1, D_MODEL), lambda b, t: (0, 0)))
        args.append(final_g)
    if has_past:
        in_specs.append(pl.BlockSpec((None, n_seg, tail, 2 * D_FF), lambda b, t: (l, b, 0, 0)))
        args.append(past)
    in_specs.append(pl.BlockSpec(memory_space=pl.ANY))
    args.append(state_out)
    y, cf = pl.pallas_call(
        functools.partial(_ffn_kernel, tm, n_t, n_seg, has_past, final_norm),
        grid=(n_grp, n_t),
        in_specs=in_specs,
        out_specs=[
            pl.BlockSpec((None, tm, D_MODEL), lambda b, t: (b, t, 0)),
            pl.BlockSpec((None, n_seg, tail, 2 * D_FF), lambda b, t: (l, b, 0, 0)),
        ],
        out_shape=[
            jax.ShapeDtypeStruct((n_grp, n_seg * n_tok, D_MODEL), F32),
            jax.ShapeDtypeStruct(state_out.shape, state_out.dtype),
        ],
        input_output_aliases={len(args) - 1: 1},
        scratch_shapes=[pltpu.VMEM((n_seg * SUBLANES, 2 * D_FF), F32)],
        compiler_params=pltpu.CompilerParams(
            dimension_semantics=("parallel", "arbitrary"), vmem_limit_bytes=VMEM_LIMIT_BYTES),
        name=f"ffn_l{l}_{'sample' if has_past else 'prompt'}",
    )(*args)
    return y.reshape(n_seq, n_tok, D_MODEL), cf


def _bias_rows(table):
    idx = np.clip(BAND - np.arange(BIAS_W), -MAX_REL, MAX_REL) + MAX_REL
    return jnp.transpose(table[:, idx, :], (0, 2, 1))


def _block_diag(w, width):
    depth, n, i, j = w.shape
    per = width // i
    eye = jnp.eye(per, dtype=w.dtype)
    w = w.reshape(depth, n // per, per, i, j)
    return (w[:, :, :, :, None, :] * eye[None, None, :, None, :, None]).reshape(depth, n // per, width, width)


def kernel(x_prompt, x_sample, c_prompt, c_sample, cache_k, cache_v, state_rnn_conv, state_rnn_h,
           state_ffn_conv, mod_mix_w, mod_mix_b, norm_mix_g, w_in, rel_bias_table, rnn_conv_w,
           rnn_conv_b, rnn_gate_a_w, rnn_gate_a_b, rnn_gate_x_w, rnn_gate_x_b, rnn_lambda, w_branch,
           w_out, mod_ffn_w, mod_ffn_b, norm_ffn_g, ffn_up_w, ffn_conv_w, ffn_conv_b, ffn_down_w,
           final_norm_g):
    n_prompt = x_prompt.shape[0]
    n_sample = x_sample.shape[0]

    rows = n_sample + n_prompt
    pad = (-rows) % SUBLANES
    c_all = jnp.concatenate([c_sample, c_prompt, jnp.zeros((pad, D_MODEL), F32)], axis=0)
    mod_mix = _modulation(c_all, mod_mix_w, mod_mix_b)
    mod_ffn = _modulation(c_all, mod_ffn_w, mod_ffn_b)

    w_in_b = w_in.astype(BF16)
    w_branch_b = w_branch.astype(BF16)
    w_out_b = w_out.astype(BF16)
    w_up_b = ffn_up_w.astype(BF16)
    w_down_b = ffn_down_w.astype(BF16)
    gate_w = jnp.concatenate([_block_diag(rnn_gate_a_w, RNN_GATE_BLK), _block_diag(rnn_gate_x_w, RNN_GATE_BLK)],
                             axis=1).astype(BF16)
    gate_b = jnp.concatenate([rnn_gate_a_b, rnn_gate_x_b], axis=-1).reshape(DEPTH, 1, 2 * D_RNN)
    bias_rows = _bias_rows(rel_bias_table)
    norm_mix = norm_mix_g.reshape(DEPTH, 1, D_MODEL)
    norm_ffn = norm_ffn_g.reshape(DEPTH, 1, D_MODEL)
    conv_b = rnn_conv_b.reshape(DEPTH, 1, D_RNN)
    lam = rnn_lambda.reshape(DEPTH, 1, D_RNN)
    fconv_b = ffn_conv_b.reshape(DEPTH, 1, 2 * D_FF)
    final_g = final_norm_g.reshape(1, D_MODEL)

    past_k = cache_k.reshape(DEPTH, n_sample, BAND_PAST, D_ATTN)
    past_v = cache_v.reshape(DEPTH, n_sample, BAND_PAST, D_ATTN)
    past_h = state_rnn_h.reshape(DEPTH, n_sample, 1, D_RNN)

    def run(x, mod_blk0, with_past):
        n_seq, n_tok, _ = x.shape
        keep_rows = min(BAND_PAST, n_tok)
        mixer_state = (jnp.zeros((DEPTH, n_seq, keep_rows, D_ATTN), F32),
                       jnp.zeros((DEPTH, n_seq, keep_rows, D_ATTN), F32),
                       jnp.zeros((DEPTH, n_seq, CONV_W - 1, D_RNN), F32),
                       jnp.zeros((DEPTH, n_seq, 1, D_RNN), F32))
        ffn_state = jnp.zeros((DEPTH, n_seq, FFN_CONV_W - 1, 2 * D_FF), F32)
        for l in range(DEPTH):
            past = (past_k, past_v, state_rnn_conv, past_h) if with_past else None
            x, *mixer_state = _mixer_layer(
                l, x, mod_mix, mod_blk0, norm_mix, w_in_b, bias_rows, rnn_conv_w, conv_b, gate_w, gate_b,
                lam, w_branch_b, w_out_b, past, mixer_state)
            x, ffn_state = _ffn_layer(
                l, x, mod_ffn, mod_blk0, norm_ffn, w_up_b, ffn_conv_w, fconv_b, w_down_b,
                final_g if l == DEPTH - 1 else None, state_ffn_conv if with_past else None, ffn_state)
        k_all, v_all, cr_all, h_all = mixer_state
        heads = (DEPTH, n_seq, keep_rows, N_HEADS, HEAD_DIM)
        return x, k_all.reshape(heads), v_all.reshape(heads), cr_all, h_all[:, :, 0, :], ffn_state

    y_s, k_s, v_s, rc_s, h_s, fc_s = run(x_sample, 0, True)
    y_p, k_p, v_p, rc_p, h_p, fc_p = run(x_prompt, n_sample, False)
    return (y_p, y_s, k_p, v_p, k_s, v_s, rc_p, rc_s, h_p, h_s, fc_p, fc_s)
```
